```python
import math
import jax, jax.numpy as jnp
from jax import lax
import numpy as np

D_MODEL = 1024
BATCH = 16
SEQ = 4096
DEPTH = 1
DEC_BATCH = 2
DEC_SEQ = 8192
PAST_LEN = 128

H_RET = 8
DK_RET = 64
DV_RET = 64
RET_CHUNK = 128
H_MLA = 4
MLA_Q_RANK = 256
MLA_KV_RANK = 128
MLA_NOPE = 128
MLA_ROPE = 64
MLA_V = 128
Q_BLOCK = 128
ROPE_BASE = 10000.0
D_MIX = H_RET * DV_RET + H_MLA * MLA_V
D_IN = 2 * H_RET * DK_RET + 2 * H_RET * DV_RET + MLA_Q_RANK + MLA_KV_RANK + MLA_ROPE
N_GROUPS = 4
EXPERTS_PER_GROUP = 8
N_EXPERTS = N_GROUPS * EXPERTS_PER_GROUP
TOP_K = 2
D_EXPERT = 512
MOE_BLOCK = 128
EPS = 1e-6

kernel_name = 'hymba_retnet_mla_hmoe_encoder'


def _rmsnorm(x, g):
    xf = x.astype(jnp.float32)
    y = xf * lax.rsqrt(jnp.mean(xf * xf, axis=-1, keepdims=True) + EPS)
    return (y * g.astype(jnp.float32)).astype(x.dtype)


def _rope_tables(seq, dim):
    inv = 1.0 / (ROPE_BASE ** (jnp.arange(0, dim, 2, dtype=jnp.float32) / dim))
    ang = jnp.arange(seq, dtype=jnp.float32)[:, None] * inv[None, :]
    return jnp.cos(ang), jnp.sin(ang)


def _apply_rope(x, cos, sin):
    half = x.shape[-1] // 2
    xf = x.astype(jnp.float32)
    x1, x2 = xf[..., :half], xf[..., half:]
    c = cos[None, :, None, :]
    s = sin[None, :, None, :]
    return jnp.concatenate([x1 * c - x2 * s, x2 * c + x1 * s], axis=-1).astype(x.dtype)


def _excl_scan(s_chunks, decay_c, reverse):
    def step(carry, s_n):
        return carry * decay_c[None, :, None, None] + s_n, carry
    _, out = lax.scan(step, jnp.zeros_like(s_chunks[0]), s_chunks, reverse=reverse)
    return out


def _retention_bidir(q, k, v, log_decay):
    B, S, H, dk = q.shape
    dv = v.shape[-1]
    C = RET_CHUNK
    N = S // C
    lg = -jnp.exp(log_decay.astype(jnp.float32))
    lf, lb = lg[0], lg[1]
    idx = jnp.arange(C, dtype=jnp.float32)
    diff = idx[:, None] - idx[None, :]
    d_f = jnp.where(diff >= 0, jnp.exp(lf[:, None, None] * jnp.maximum(diff, 0.0)), 0.0)
    d_b = jnp.where(diff < 0, jnp.exp(lb[:, None, None] * jnp.maximum(-diff, 0.0)), 0.0)
    dmat = d_f + d_b
    qc = q.reshape(B, N, C, H, dk)
    kc = k.reshape(B, N, C, H, dk)
    vc = v.reshape(B, N, C, H, dv)
    scores = jnp.einsum('bnihd,bnjhd->bnhij', qc, kc, preferred_element_type=jnp.float32) * dmat[None, None]
    intra = jnp.einsum('bnhij,bnjhe->bnihe', scores, vc.astype(jnp.float32))
    w_kf = jnp.exp(lf[None, :] * (C - 1 - idx)[:, None])
    w_kb = jnp.exp(lb[None, :] * idx[:, None])
    s_f = jnp.einsum('bnjhd,bnjhe->nbhde', kc * w_kf[None, None, :, :, None], vc.astype(jnp.float32))
    s_b = jnp.einsum('bnjhd,bnjhe->nbhde', kc * w_kb[None, None, :, :, None], vc.astype(jnp.float32))
    st_f = _excl_scan(s_f, jnp.exp(lf * C), reverse=False)
    st_b = _excl_scan(s_b, jnp.exp(lb * C), reverse=True)
    q_f = qc * jnp.exp(lf[None, :] * (idx + 1.0)[:, None])[None, None, :, :, None]
    q_b = qc * jnp.exp(lb[None, :] * (C - idx)[:, None])[None, None, :, :, None]
    inter = jnp.einsum('bnihd,nbhde->bnihe', q_f, st_f) + jnp.einsum('bnihd,nbhde->bnihe', q_b, st_b)
    return (intra + inter).reshape(B, S, H, dv)


def _mla_attention(qn, qr, kn, kr, v):
    B, S, H, dn = qn.shape
    dr = qr.shape[-1]
    dv = v.shape[-1]
    nq = S // Q_BLOCK
    scale = (dn + dr) ** -0.5
    qn_b = qn.reshape(B, nq, Q_BLOCK, H, dn).swapaxes(0, 1)
    qr_b = qr.reshape(B, nq, Q_BLOCK, H, dr).swapaxes(0, 1)

    def attend(args):
        qnb, qrb = args
        s = (jnp.einsum('bqhd,bkhd->bhqk', qnb, kn, preferred_element_type=jnp.float32)
             + jnp.einsum('bqhr,bkr->bhqk', qrb, kr, preferred_element_type=jnp.float32)) * scale
        p = jax.nn.softmax(s, axis=-1)
        return jnp.einsum('bhqk,bkhd->bqhd', p.astype(v.dtype), v)

    o = lax.map(attend, (qn_b, qr_b))
    return o.swapaxes(0, 1).reshape(B, S, H * dv)


def _hier_moe(h, w_rg, w_re, w_g, w_u, w_d):
    B, S, D = h.shape
    T = B * S
    xt = h.reshape(T, D)
    pg = jax.nn.softmax(jnp.einsum('td,dg->tg', xt, w_rg, preferred_element_type=jnp.float32), axis=-1)
    g_sel = jnp.argmax(pg, axis=-1)
    p_sel = jnp.take_along_axis(pg, g_sel[:, None], axis=1)
    le = jnp.einsum('td,de->te', xt, w_re, preferred_element_type=jnp.float32).reshape(T, N_GROUPS, EXPERTS_PER_GROUP)
    le_sel = jnp.take_along_axis(le, g_sel[:, None, None], axis=1)[:, 0]
    pe = jax.nn.softmax(le_sel, axis=-1)
    top_p, top_i = lax.top_k(pe, TOP_K)
    top_p = top_p / jnp.sum(top_p, axis=-1, keepdims=True)
    gate = p_sel * top_p
    eid = g_sel[:, None] * EXPERTS_PER_GROUP + top_i
    A = T * TOP_K
    flat_e = eid.reshape(A)
    flat_tok = jnp.repeat(jnp.arange(T, dtype=jnp.int32), TOP_K)
    flat_w = gate.reshape(A)
    order = jnp.argsort(flat_e)
    se = flat_e[order]
    counts = jnp.bincount(flat_e, length=N_EXPERTS)
    starts = jnp.cumsum(counts) - counts
    pcounts = (counts + MOE_BLOCK - 1) // MOE_BLOCK * MOE_BLOCK
    pends = jnp.cumsum(pcounts)
    pstarts = pends - pcounts
    dest = pstarts[se] + (jnp.arange(A, dtype=jnp.int32) - starts[se])
    P = (A + MOE_BLOCK - 1) // MOE_BLOCK * MOE_BLOCK + N_EXPERTS * MOE_BLOCK
    nb = P // MOE_BLOCK
    tok_sorted = flat_tok[order]
    x_pad = jnp.zeros((P, D), h.dtype).at[dest].set(xt[tok_sorted])
    blk_e = jnp.minimum(jnp.searchsorted(pends, jnp.arange(nb, dtype=pends.dtype) * MOE_BLOCK, side='right'), N_EXPERTS - 1)

    def expert_block(args):
        xb, e = args
        return (jax.nn.silu(xb @ w_g[e]) * (xb @ w_u[e])) @ w_d[e]

    y_pad = lax.map(expert_block, (x_pad.reshape(nb, MOE_BLOCK, D), blk_e)).reshape(P, D)
    contrib = (y_pad[dest] * flat_w[order][:, None]).astype(h.dtype)
    y = jnp.zeros((T, D), h.dtype).at[tok_sorted].add(contrib)
    return y.reshape(B, S, D)


def _layer(x, g_mix, w_in, ret_log_decay, ret_gn_g, mla_q_norm_g, w_q_up, mla_kv_norm_g, w_kv_up,
           w_out, g_ffn, w_router_group, w_router_expert, w_exp_gate, w_exp_up, w_exp_down):
    B, S, _ = x.shape
    h = _rmsnorm(x, g_mix)
    proj = jnp.einsum('bsd,de->bse', h, w_in)
    sizes = [H_RET * DK_RET, H_RET * DK_RET, H_RET * DV_RET, H_RET * DV_RET, MLA_Q_RANK, MLA_KV_RANK, MLA_ROPE]
    cuts = []
    acc = 0
    for sz in sizes[:-1]:
        acc += sz
        cuts.append(acc)
    rq, rk, rv, rg, cq, ckv, kr = jnp.split(proj, cuts, axis=-1)
    cos_r, sin_r = _rope_tables(S, DK_RET)
    q = _apply_rope(rq.reshape(B, S, H_RET, DK_RET), cos_r, sin_r)
    k = _apply_rope(rk.reshape(B, S, H_RET, DK_RET), cos_r, sin_r) * (DK_RET ** -0.5)
    v = rv.reshape(B, S, H_RET, DV_RET)
    ret = _retention_bidir(q, k, v, ret_log_decay)
    mu = jnp.mean(ret, axis=-1, keepdims=True)
    var = jnp.mean(jnp.square(ret - mu), axis=-1, keepdims=True)
    ret = ((ret - mu) * lax.rsqrt(var + EPS)).reshape(B, S, H_RET * DV_RET) * ret_gn_g.astype(jnp.float32)
    ret_out = (jax.nn.silu(rg.astype(jnp.float32)) * ret).astype(x.dtype)
    cos_m, sin_m = _rope_tables(S, MLA_ROPE)
    qh = jnp.einsum('bsr,re->bse', _rmsnorm(cq, mla_q_norm_g), w_q_up).reshape(B, S, H_MLA, MLA_NOPE + MLA_ROPE)
    qn = qh[..., :MLA_NOPE]
    qr = _apply_rope(qh[..., MLA_NOPE:], cos_m, sin_m)
    kvh = jnp.einsum('bsr,re->bse', _rmsnorm(ckv, mla_kv_norm_g), w_kv_up).reshape(B, S, H_MLA, MLA_NOPE + MLA_V)
    kn = kvh[..., :MLA_NOPE]
    vm = kvh[..., MLA_NOPE:]
    krr = _apply_rope(kr.reshape(B, S, 1, MLA_ROPE), cos_m, sin_m)[:, :, 0]
    mla_out = _mla_attention(qn, qr, kn, krr, vm)
    mix = jnp.concatenate([ret_out, mla_out], axis=-1)
    x = x + jnp.einsum('bse,ed->bsd', mix, w_out)
    x = x + _hier_moe(_rmsnorm(x, g_ffn), w_router_group, w_router_expert, w_exp_gate, w_exp_up, w_exp_down)
    return x


def setup_inputs(seed: int = 0) -> dict:
    key = jax.random.key(seed)
    ks = jax.random.split(key, 20)
    f32 = jnp.float32
    nrm = lambda k, shape, scale: jax.random.normal(k, shape, f32) * scale
    base_decay = -(5.0 + jnp.arange(H_RET, dtype=f32)) * math.log(2.0)
    return {
        'x_prompt': nrm(ks[0], (BATCH, SEQ, D_MODEL), 1.0),
        'x_sample': nrm(ks[1], (DEC_BATCH, DEC_SEQ, D_MODEL), 1.0),
        'g_mix': 1.0 + nrm(ks[2], (DEPTH, D_MODEL), 0.01),
        'w_in': nrm(ks[3], (DEPTH, D_MODEL, D_IN), D_MODEL ** -0.5),
        'ret_log_decay': base_decay[None, None, :] + nrm(ks[4], (DEPTH, 2, H_RET), 0.1),
        'ret_gn_g': 1.0 + nrm(ks[5], (DEPTH, H_RET * DV_RET), 0.01),
        'mla_q_norm_g': 1.0 + nrm(ks[6], (DEPTH, MLA_Q_RANK), 0.01),
        'w_q_up': nrm(ks[7], (DEPTH, MLA_Q_RANK, H_MLA * (MLA_NOPE + MLA_ROPE)), MLA_Q_RANK ** -0.5),
        'mla_kv_norm_g': 1.0 + nrm(ks[8], (DEPTH, MLA_KV_RANK), 0.01),
        'w_kv_up': nrm(ks[9], (DEPTH, MLA_KV_RANK, H_MLA * (MLA_NOPE + MLA_V)), MLA_KV_RANK ** -0.5),
        'w_out': nrm(ks[10], (DEPTH, D_MIX, D_MODEL), D_MIX ** -0.5),
        'g_ffn': 1.0 + nrm(ks[11], (DEPTH, D_MODEL), 0.01),
        'w_router_group': nrm(ks[12], (DEPTH, D_MODEL, N_GROUPS), D_MODEL ** -0.5),
        'w_router_expert': nrm(ks[13], (DEPTH, D_MODEL, N_EXPERTS), D_MODEL ** -0.5),
        'w_exp_gate': nrm(ks[14], (DEPTH, N_EXPERTS, D_MODEL, D_EXPERT), D_MODEL ** -0.5),
        'w_exp_up': nrm(ks[15], (DEPTH, N_EXPERTS, D_MODEL, D_EXPERT), D_MODEL ** -0.5),
        'w_exp_down': nrm(ks[16], (DEPTH, N_EXPERTS, D_EXPERT, D_MODEL), D_EXPERT ** -0.5),
        'g_final': 1.0 + nrm(ks[17], (D_MODEL,), 0.01),
    }


def reference(x_prompt, x_sample, g_mix, w_in, ret_log_decay, ret_gn_g, mla_q_norm_g, w_q_up,
              mla_kv_norm_g, w_kv_up, w_out, g_ffn, w_router_group, w_router_expert,
              w_exp_gate, w_exp_up, w_exp_down, g_final):
    def trunk(x):
        for l in range(DEPTH):
            x = _layer(x, g_mix[l], w_in[l], ret_log_decay[l], ret_gn_g[l], mla_q_norm_g[l], w_q_up[l],
                       mla_kv_norm_g[l], w_kv_up[l], w_out[l], g_ffn[l], w_router_group[l],
                       w_router_expert[l], w_exp_gate[l], w_exp_up[l], w_exp_down[l])
        return _rmsnorm(x, g_final)

    y_prompt = trunk(x_prompt)
    y_sample = trunk(x_sample)
    return (y_prompt, y_sample)
```

```python
import functools
import math

import jax
import jax.numpy as jnp
from jax import lax
from jax.experimental import pallas as pl
from jax.experimental.pallas import tpu as pltpu

F32 = jnp.float32
BF16 = jnp.bfloat16
I32 = jnp.int32

D_MODEL = 1024
H_RET = 8
DK_RET = 64
DV_RET = 64
RET_CHUNK = 128
H_MLA = 4
MLA_Q_RANK = 256
MLA_KV_RANK = 128
MLA_NOPE = 128
MLA_ROPE = 64
MLA_V = 128
ROPE_BASE = 10000.0
N_GROUPS = 4
EXPERTS_PER_GROUP = 8
N_EXPERTS = N_GROUPS * EXPERTS_PER_GROUP
D_EXPERT = 512
EPS = 1e-6

LANES = 128
N_PAIRS = H_RET // 2
D_RET = H_RET * DK_RET
D_QK_MLA = 2 * LANES
D_IN_PAD = 4 * D_RET + MLA_Q_RANK + MLA_KV_RANK + LANES
ROUTER_COLS = LANES
VMEM_LIMIT = 56 * 1024 * 1024


def _tiles(seq):
    tm = min(512, seq)
    return dict(
        tm=tm,
        tq=min(256, seq),
        tk=tm,
        ch=min(4, seq // RET_CHUNK),
        bm=256,
        tmd=min(256, seq),
        tp=min(2048, seq),
    )


def _mm(a, b):
    return jnp.dot(a, b, preferred_element_type=F32)


def _mm_nt(a, b):
    return lax.dot_general(a, b, (((1,), (1,)), ((), ())), preferred_element_type=F32)


def _rms(x, g):
    return x * lax.rsqrt(jnp.mean(x * x, axis=-1, keepdims=True) + EPS) * g


def _rope_half(x, c, s):
    return x * c + pltpu.roll(x, 64, 1) * s


def _inproj_body(x_ref, gmix_ref, win_ref, cr_ref, sr_ref, cm_ref, sm_ref, gq_ref, wq_ref,
                 gkv_ref, wkn_ref, wvt_ref,
                 q_ref, k_ref, v_ref, sg_ref, qf_ref, kf_ref, vt_ref):
    h = _rms(x_ref[0], gmix_ref[...]).astype(BF16)

    def proj(lo, hi):
        return _mm(h, win_ref[:, lo:hi])

    cr, sr = cr_ref[...], sr_ref[...]
    q = proj(0, D_RET)
    k = proj(D_RET, 2 * D_RET)
    for p in range(N_PAIRS):
        sl = slice(p * LANES, (p + 1) * LANES)
        q_ref[0, :, sl] = _rope_half(q[:, sl], cr, sr).astype(BF16)
        k_ref[0, :, sl] = (_rope_half(k[:, sl], cr, sr) * (DK_RET ** -0.5)).astype(BF16)
    v_ref[0] = proj(2 * D_RET, 3 * D_RET).astype(BF16)
    g = proj(3 * D_RET, 4 * D_RET)
    sg_ref[0] = (g * jax.nn.sigmoid(g)).astype(BF16)

    c = proj(4 * D_RET, D_IN_PAD)
    cq = c[:, :MLA_Q_RANK]
    ckv = c[:, MLA_Q_RANK:MLA_Q_RANK + MLA_KV_RANK]
    kr = c[:, MLA_Q_RANK + MLA_KV_RANK:]
    cm, sm = cm_ref[...], sm_ref[...]
    scale = (MLA_NOPE + MLA_ROPE) ** -0.5

    qh = _mm(_rms(cq, gq_ref[...]).astype(BF16), wq_ref[...])
    for hd in range(H_MLA):
        base = hd * D_QK_MLA
        qf_ref[0, hd, :, :LANES] = (qh[:, base:base + LANES] * scale).astype(BF16)
        qr = _rope_half(qh[:, base + LANES:base + D_QK_MLA], cm, sm)
        qf_ref[0, hd, :, LANES:] = (qr * scale).astype(BF16)

    ckvn = _rms(ckv, gkv_ref[...]).astype(BF16)
    kn = _mm(ckvn, wkn_ref[...])
    krr = _rope_half(kr, cm, sm).astype(BF16)
    vt = _mm_nt(wvt_ref[...], ckvn)
    for hd in range(H_MLA):
        sl = slice(hd * LANES, (hd + 1) * LANES)
        kf_ref[0, hd, :, :LANES] = kn[:, sl].astype(BF16)
        kf_ref[0, hd, :, LANES:] = krr
        vt_ref[0, hd, 0] = vt[sl, :].astype(BF16)


def _inproj(x, gmix, win, cr, sr, cm, sm, gq, wq, gkv, wkn, wvt, t):
    B, S, D = x.shape
    tm = t["tm"]
    ns = S // tm
    tok = lambda b, j: (b, j, 0)
    full = lambda b, j: (0, 0)
    tab = lambda b, j: (j, 0)
    head = lambda b, j: (b, 0, j, 0)
    out_shape = (
        jax.ShapeDtypeStruct((B, S, D_RET), BF16),
        jax.ShapeDtypeStruct((B, S, D_RET), BF16),
        jax.ShapeDtypeStruct((B, S, D_RET), BF16),
        jax.ShapeDtypeStruct((B, S, D_RET), BF16),
        jax.ShapeDtypeStruct((B, H_MLA, S, D_QK_MLA), BF16),
        jax.ShapeDtypeStruct((B, H_MLA, S, D_QK_MLA), BF16),
        jax.ShapeDtypeStruct((B, H_MLA, ns, MLA_V, tm), BF16),
    )
    return pl.pallas_call(
        _inproj_body,
        grid=(B, ns),
        in_specs=[
            pl.BlockSpec((1, tm, D), tok),
            pl.BlockSpec((1, D), full),
            pl.BlockSpec((D, D_IN_PAD), full),
            pl.BlockSpec((tm, LANES), tab), pl.BlockSpec((tm, LANES), tab),
            pl.BlockSpec((tm, LANES), tab), pl.BlockSpec((tm, LANES), tab),
            pl.BlockSpec((1, MLA_Q_RANK), full),
            pl.BlockSpec((MLA_Q_RANK, H_MLA * D_QK_MLA), full),
            pl.BlockSpec((1, MLA_KV_RANK), full),
            pl.BlockSpec((MLA_KV_RANK, H_MLA * MLA_NOPE), full),
            pl.BlockSpec((H_MLA * MLA_V, MLA_KV_RANK), full),
        ],
        out_specs=(
            pl.BlockSpec((1, tm, D_RET), tok), pl.BlockSpec((1, tm, D_RET), tok),
            pl.BlockSpec((1, tm, D_RET), tok), pl.BlockSpec((1, tm, D_RET), tok),
            pl.BlockSpec((1, H_MLA, tm, D_QK_MLA), head),
            pl.BlockSpec((1, H_MLA, tm, D_QK_MLA), head),
            pl.BlockSpec((1, H_MLA, 1, MLA_V, tm), lambda b, j: (b, 0, j, 0, 0)),
        ),
        out_shape=out_shape,
        compiler_params=pltpu.CompilerParams(
            dimension_semantics=("parallel", "parallel"), vmem_limit_bytes=VMEM_LIMIT),
        name="inproj",
    )(x, gmix, win, cr, sr, cm, sm, gq, wq, gkv, wkn, wvt)


def _pair_masks():
    lane = lax.broadcasted_iota(I32, (1, LANES), 1)
    khead = (lane // 32) % 2
    vhead = lane // DV_RET
    row = lax.broadcasted_iota(I32, (LANES, LANES), 0)
    col = lax.broadcasted_iota(I32, (LANES, LANES), 1)
    block = ((row // 32) % 2) == (col // DV_RET)
    return khead, vhead, block


def _chunk_state(kp, vp, kscale, block):
    kw = (kp.astype(F32) * kscale).T.astype(BF16)
    s = _mm(kw, vp)
    return jnp.where(block, s, 0.0)


def _ret_bwd_body(k_ref, v_ref, kbs_ref, decb_ref, stb_ref, carry_ref, *, ch):
    @pl.when(pl.program_id(1) == 0)
    def _():
        carry_ref[...] = jnp.zeros_like(carry_ref)

    _, _, block = _pair_masks()
    for cc in reversed(range(ch)):
        rows = slice(cc * RET_CHUNK, (cc + 1) * RET_CHUNK)
        for p in range(N_PAIRS):
            sl = slice(p * LANES, (p + 1) * LANES)
            stb_ref[0, cc, p] = carry_ref[p].astype(BF16)
            s_b = _chunk_state(k_ref[0, rows, sl], v_ref[0, rows, sl], kbs_ref[:, sl], block)
            carry_ref[p] = carry_ref[p] * decb_ref[p] + s_b


def _ret_fwd_body(q_ref, k_ref, v_ref, sg_ref, stb_ref, qfs_ref, qbs_ref, kfs_ref, dm_ref,
                  decf_ref, gn_ref, o_ref, carry_ref, *, ch):
    @pl.when(pl.program_id(1) == 0)
    def _():
        carry_ref[...] = jnp.zeros_like(carry_ref)

    khead, vhead, block = _pair_masks()
    zero_bf = jnp.zeros((), BF16)
    for cc in range(ch):
        rows = slice(cc * RET_CHUNK, (cc + 1) * RET_CHUNK)
        for p in range(N_PAIRS):
            sl = slice(p * LANES, (p + 1) * LANES)
            qp, kp, vp = q_ref[0, rows, sl], k_ref[0, rows, sl], v_ref[0, rows, sl]
            k2 = jnp.concatenate([jnp.where(khead == 0, kp, zero_bf),
                                  jnp.where(khead == 1, kp, zero_bf)], axis=0)
            sc = _mm_nt(qp, k2) * dm_ref[p]
            v2 = jnp.concatenate([jnp.where(vhead == 0, vp, zero_bf),
                                  jnp.where(vhead == 1, vp, zero_bf)], axis=0)
            intra = _mm(sc.astype(BF16), v2)
            qf = qp.astype(F32)
            qfb = jnp.concatenate([(qf * qfs_ref[:, sl]).astype(BF16),
                                   (qf * qbs_ref[:, sl]).astype(BF16)], axis=1)
            st2 = jnp.concatenate([carry_ref[p].astype(BF16), stb_ref[0, cc, p]], axis=0)
            o = intra + _mm(qfb, st2)
            m0 = vhead == 0
            mu = jnp.where(m0,
                           jnp.sum(jnp.where(m0, o, 0.0), axis=-1, keepdims=True),
                           jnp.sum(jnp.where(m0, 0.0, o), axis=-1, keepdims=True)) * (1.0 / DV_RET)
            d = o - mu
            d2 = d * d
            var = jnp.where(m0,
                            jnp.sum(jnp.where(m0, d2, 0.0), axis=-1, keepdims=True),
                            jnp.sum(jnp.where(m0, 0.0, d2), axis=-1, keepdims=True)) * (1.0 / DV_RET)
            y = d * lax.rsqrt(var + EPS) * gn_ref[:, sl]
            o_ref[0, rows, sl] = (sg_ref[0, rows, sl].astype(F32) * y).astype(BF16)
            s_f = _chunk_state(kp, vp, kfs_ref[:, sl], block)
            carry_ref[p] = carry_ref[p] * decf_ref[p] + s_f


def _retention(q, k, v, sg, tabs, gn, t):
    B, S, _ = q.shape
    ch = t["ch"]
    rows = ch * RET_CHUNK
    ns = S // rows
    n_chunks = S // RET_CHUNK
    qfs, qbs, kfs, kbs, dm, decf, decb = tabs
    tab2 = pl.BlockSpec((RET_CHUNK, D_RET), lambda b, j: (0, 0))
    tab3 = pl.BlockSpec((N_PAIRS, LANES, LANES), lambda b, j: (0, 0, 0))
    params = pltpu.CompilerParams(dimension_semantics=("parallel", "arbitrary"),
                                  vmem_limit_bytes=VMEM_LIMIT)
    rev = lambda b, j: (b, ns - 1 - j, 0)
    stb = pl.pallas_call(
        functools.partial(_ret_bwd_body, ch=ch),
        grid=(B, ns),
        in_specs=[pl.BlockSpec((1, rows, D_RET), rev), pl.BlockSpec((1, rows, D_RET), rev),
                  tab2, tab3],
        out_specs=pl.BlockSpec((1, ch, N_PAIRS, LANES, LANES),
                               lambda b, j: (b, ns - 1 - j, 0, 0, 0)),
        out_shape=jax.ShapeDtypeStruct((B, n_chunks, N_PAIRS, LANES, LANES), BF16),
        scratch_shapes=[pltpu.VMEM((N_PAIRS, LANES, LANES), F32)],
        compiler_params=params,
        name="ret_bwd",
    )(k, v, kbs, decb)
    fwd = lambda b, j: (b, j, 0)
    tok = pl.BlockSpec((1, rows, D_RET), fwd)
    return pl.pallas_call(
        functools.partial(_ret_fwd_body, ch=ch),
        grid=(B, ns),
        in_specs=[tok, tok, tok, tok,
                  pl.BlockSpec((1, ch, N_PAIRS, LANES, LANES), lambda b, j: (b, j, 0, 0, 0)),
                  tab2, tab2, tab2,
                  pl.BlockSpec((N_PAIRS, RET_CHUNK, 2 * RET_CHUNK), lambda b, j: (0, 0, 0)),
                  tab3,
                  pl.BlockSpec((1, D_RET), lambda b, j: (0, 0))],
        out_specs=tok,
        out_shape=jax.ShapeDtypeStruct((B, S, D_RET), BF16),
        scratch_shapes=[pltpu.VMEM((N_PAIRS, LANES, LANES), F32)],
        compiler_params=params,
        name="ret_fwd",
    )(q, k, v, sg, stb, qfs, qbs, kfs, dm, decf, gn)


def _flash_body(q_ref, k_ref, vt_ref, o_ref, *, nk, tk):
    q = q_ref[0, 0]
    tq = q.shape[0]

    def step(c, carry):
        m, l, acc = carry
        kc = k_ref[0, 0, pl.ds(pl.multiple_of(c * tk, tk), tk), :]
        s = _mm_nt(kc, q)
        m_new = jnp.maximum(m, jnp.max(s, axis=0, keepdims=True))
        alpha = jnp.exp(m - m_new)
        p = jnp.exp(s - m_new)
        l = alpha * l + jnp.sum(p, axis=0, keepdims=True)
        pv = _mm(vt_ref[0, 0, c], p.astype(BF16))
        return m_new, l, alpha * acc + pv

    init = (jnp.full((1, tq), -jnp.inf, F32), jnp.zeros((1, tq), F32), jnp.zeros((MLA_V, tq), F32))
    _, l, acc = lax.fori_loop(0, nk, step, init)
    o_ref[0] = (acc / l).T.astype(BF16)


def _flash(qf, kf, vt, t):
    B, H, S, _ = qf.shape
    tq, tk = t["tq"], t["tk"]
    nk = S // tk
    return pl.pallas_call(
        functools.partial(_flash_body, nk=nk, tk=tk),
        grid=(B, H, S // tq),
        in_specs=[pl.BlockSpec((1, 1, tq, D_QK_MLA), lambda b, h, i: (b, h, i, 0)),
                  pl.BlockSpec((1, 1, S, D_QK_MLA), lambda b, h, i: (b, h, 0, 0)),
                  pl.BlockSpec((1, 1, nk, MLA_V, tk), lambda b, h, i: (b, h, 0, 0, 0))],
        out_specs=pl.BlockSpec((1, tq, MLA_V), lambda b, h, i: (b, i, h)),
        out_shape=jax.ShapeDtypeStruct((B, S, H * MLA_V), BF16),
        compiler_params=pltpu.CompilerParams(
            dimension_semantics=("parallel", "parallel", "parallel"), vmem_limit_bytes=VMEM_LIMIT),
        name="flash",
    )(qf, kf, vt)


def _outproj_router_body(x_ref, ret_ref, mla_ref, wo_ref, gffn_ref, wr_ref,
                         x1_ref, ri_ref, rg_ref, cnt_ref, tri_ref, carry_ref):
    tm = x_ref.shape[0]

    @pl.when(pl.program_id(0) == 0)
    def _():
        carry_ref[...] = jnp.zeros_like(carry_ref)
        r = lax.broadcasted_iota(I32, (tm, tm), 0)
        c = lax.broadcasted_iota(I32, (tm, tm), 1)
        tri_ref[...] = (r < c).astype(BF16)

    x1 = (x_ref[...]
          + _mm(ret_ref[...], wo_ref[:D_RET, :]) + _mm(mla_ref[...], wo_ref[D_RET:, :]))
    x1_ref[...] = x1
    h2 = _rms(x1, gffn_ref[...]).astype(BF16)
    lt = _mm(h2, wr_ref[...]).T

    row8 = lax.broadcasted_iota(I32, (8, tm), 0)
    lg = jnp.where(row8 < N_GROUPS, lt[0:8], -jnp.inf)
    eg = jnp.exp(lg - jnp.max(lg, axis=0, keepdims=True))
    pg = eg / jnp.sum(eg, axis=0, keepdims=True)
    p_sel = jnp.max(pg, axis=0, keepdims=True)
    g_sel = jnp.min(jnp.where(pg == p_sel, row8, 8), axis=0, keepdims=True)

    le = lt[8:16]
    for g in range(1, N_GROUPS):
        le = jnp.where(g_sel == g, lt[8 + 8 * g:16 + 8 * g], le)
    ee = jnp.exp(le - jnp.max(le, axis=0, keepdims=True))
    pe = ee / jnp.sum(ee, axis=0, keepdims=True)
    p1 = jnp.max(pe, axis=0, keepdims=True)
    i1 = jnp.min(jnp.where(pe == p1, row8, 8), axis=0, keepdims=True)
    pe2 = jnp.where(row8 == i1, -1.0, pe)
    p2 = jnp.max(pe2, axis=0, keepdims=True)
    i2 = jnp.min(jnp.where(pe2 == p2, row8, 8), axis=0, keepdims=True)
    den = p1 + p2
    e0 = g_sel * EXPERTS_PER_GROUP + i1
    e1 = g_sel * EXPERTS_PER_GROUP + i2

    rowe = lax.broadcasted_iota(I32, (N_EXPERTS, tm), 0)
    oh0 = rowe == e0
    oh1 = rowe == e1
    cnt = jnp.where(oh0 | oh1, 1.0, 0.0)
    base = carry_ref[:, 0:1] + _mm(cnt.astype(BF16), tri_ref[...])
    r0 = jnp.sum(jnp.where(oh0, base, 0.0), axis=0, keepdims=True)
    r1 = jnp.sum(jnp.where(oh1, base, 0.0), axis=0, keepdims=True)
    carry_ref[...] = carry_ref[...] + jnp.sum(cnt, axis=1, keepdims=True)

    zi = jnp.zeros((4, tm), I32)
    ri_ref[...] = jnp.concatenate([e0, e1, r0.astype(I32), r1.astype(I32), zi], axis=0)
    zf = jnp.zeros((6, tm), F32)
    rg_ref[...] = jnp.concatenate([p_sel * (p1 / den), p_sel * (p2 / den), zf], axis=0)
    cnt_ref[...] = carry_ref[...].astype(I32)


def _outproj_router(x2, ret2, mla2, wo, gffn, wr, t):
    T, D = x2.shape
    tm = t["tm"]
    tok = lambda i: (i, 0)
    full = lambda i: (0, 0)
    lane = lambda i: (0, i)
    return pl.pallas_call(
        _outproj_router_body,
        grid=(T // tm,),
        in_specs=[pl.BlockSpec((tm, D), tok), pl.BlockSpec((tm, D_RET), tok),
                  pl.BlockSpec((tm, H_MLA * MLA_V), tok),
                  pl.BlockSpec((D, D), full), pl.BlockSpec((1, D), full),
                  pl.BlockSpec((D, ROUTER_COLS), full)],
        out_specs=(pl.BlockSpec((tm, D), tok), pl.BlockSpec((8, tm), lane),
                   pl.BlockSpec((8, tm), lane), pl.BlockSpec((N_EXPERTS, LANES), full)),
        out_shape=(jax.ShapeDtypeStruct((T, D), F32), jax.ShapeDtypeStruct((8, T), I32),
                   jax.ShapeDtypeStruct((8, T), F32), jax.ShapeDtypeStruct((N_EXPERTS, LANES), I32)),
        scratch_shapes=[pltpu.VMEM((tm, tm), BF16), pltpu.VMEM((N_EXPERTS, LANES), F32)],
        compiler_params=pltpu.CompilerParams(dimension_semantics=("arbitrary",),
                                             vmem_limit_bytes=VMEM_LIMIT),
        name="outproj_router",
    )(x2, ret2, mla2, wo, gffn, wr)


def _plan_body(ri_ref, cnt_ref, dest_ref, blk_ref, *, bm, nb_pad):
    tp = ri_ref.shape[1]
    shift = int(math.log2(bm))
    cnt_col = cnt_ref[:, 0:1]
    pc_col = (((cnt_col + (bm - 1)) >> shift) << shift).astype(F32)
    r = lax.broadcasted_iota(I32, (N_EXPERTS, N_EXPERTS), 0)
    c = lax.broadcasted_iota(I32, (N_EXPERTS, N_EXPERTS), 1)
    pc_row = jnp.sum(jnp.where(r == c, pc_col, 0.0), axis=0, keepdims=True)
    pstart_col = jnp.sum(jnp.where(c < r, pc_row, 0.0), axis=1, keepdims=True)
    pend_col = pstart_col + pc_col

    rowe = lax.broadcasted_iota(I32, (N_EXPERTS, tp), 0)
    for kk in range(2):
        e = ri_ref[kk:kk + 1, :]
        off = jnp.sum(jnp.where(rowe == e, pstart_col, 0.0), axis=0, keepdims=True)
        dest_ref[kk:kk + 1, :] = off.astype(I32) + ri_ref[2 + kk:3 + kk, :]

    blk_row = (lax.broadcasted_iota(I32, (N_EXPERTS, nb_pad), 1) * bm).astype(F32)
    owner = jnp.sum(jnp.where(pend_col <= blk_row, 1.0, 0.0), axis=0, keepdims=True)
    owner = jnp.minimum(owner, N_EXPERTS - 1.0).astype(I32)
    n_used = jnp.max(pend_col, axis=0, keepdims=True).astype(I32) >> shift
    lane = lax.broadcasted_iota(I32, (1, nb_pad), 1)
    blk_ref[...] = jnp.where(lane == nb_pad - 1, n_used, owner)


def _plan(ri, cnt, nb, t):
    T = ri.shape[1]
    tp, bm = t["tp"], t["bm"]
    nb_pad = (nb + 1 + LANES - 1) // LANES * LANES
    return pl.pallas_call(
        functools.partial(_plan_body, bm=bm, nb_pad=nb_pad),
        grid=(T // tp,),
        in_specs=[pl.BlockSpec((8, tp), lambda i: (0, i)),
                  pl.BlockSpec((N_EXPERTS, LANES), lambda i: (0, 0))],
        out_specs=(pl.BlockSpec((2, tp), lambda i: (0, i)),
                   pl.BlockSpec((1, nb_pad), lambda i: (0, 0))),
        out_shape=(jax.ShapeDtypeStruct((2, T), I32), jax.ShapeDtypeStruct((1, nb_pad), I32)),
        compiler_params=pltpu.CompilerParams(dimension_semantics=("arbitrary",)),
        name="plan",
    )(ri, cnt)


def _row_copy(src, src_row, dst, dst_row, sem):
    return pltpu.make_async_copy(src.at[pl.ds(src_row, 1), :], dst.at[pl.ds(dst_row, 1), :], sem)


def _dispatch_body(dest_ref, x1_ref, gffn_ref, xpad_in_ref, xpad_ref, hbuf, sem):
    del xpad_in_ref
    tmd = x1_ref.shape[0]
    hbuf[...] = _rms(x1_ref[...], gffn_ref[...])

    def issue(i, _):
        _row_copy(hbuf, i, xpad_ref, dest_ref[0, i], sem.at[0]).start()
        _row_copy(hbuf, i, xpad_ref, dest_ref[1, i], sem.at[1]).start()
        return 0

    def drain(i, _):
        _row_copy(hbuf, i, xpad_ref, 0, sem.at[0]).wait()
        _row_copy(hbuf, i, xpad_ref, 0, sem.at[1]).wait()
        return 0

    lax.fori_loop(0, tmd, issue, 0)
    lax.fori_loop(0, tmd, drain, 0)


def _dispatch(dest, x1, gffn, xpad0, t):
    T, D = x1.shape
    tmd = t["tmd"]
    return pl.pallas_call(
        _dispatch_body,
        grid=(T // tmd,),
        in_specs=[pl.BlockSpec((2, tmd), lambda i: (0, i), memory_space=pltpu.SMEM),
                  pl.BlockSpec((tmd, D), lambda i: (i, 0)),
                  pl.BlockSpec((1, D), lambda i: (0, 0)),
                  pl.BlockSpec(memory_space=pl.ANY)],
        out_specs=pl.BlockSpec(memory_space=pl.ANY),
        out_shape=jax.ShapeDtypeStruct(xpad0.shape, F32),
        scratch_shapes=[pltpu.VMEM((tmd, D), F32), pltpu.SemaphoreType.DMA((2,))],
        input_output_aliases={3: 0},
        compiler_params=pltpu.CompilerParams(dimension_semantics=("arbitrary",)),
        name="dispatch",
    )(dest, x1, gffn, xpad0)


def _experts_body(blk_ref, x_ref, wg_ref, wu_ref, wd_ref, y_ref, *, n_used_idx):
    @pl.when(pl.program_id(0) < blk_ref[n_used_idx])
    def _():
        xb = x_ref[...].astype(BF16)
        a = _mm(xb, wg_ref[0])
        u = _mm(xb, wu_ref[0])
        hmid = (a * jax.nn.sigmoid(a) * u).astype(BF16)
        y_ref[...] = _mm(hmid, wd_ref[0])

    @pl.when(pl.program_id(0) >= blk_ref[n_used_idx])
    def _():
        y_ref[...] = jnp.zeros_like(y_ref)


def _experts(blk, xpad, wg, wu, wd, nb, t):
    P, D = xpad.shape
    bm = t["bm"]
    n_used_idx = blk.shape[0] - 1
    row = lambda j, blk: (j, 0)
    wsel = lambda j, blk: (blk[j], 0, 0)
    return pl.pallas_call(
        functools.partial(_experts_body, n_used_idx=n_used_idx),
        grid_spec=pltpu.PrefetchScalarGridSpec(
            num_scalar_prefetch=1,
            grid=(nb,),
            in_specs=[pl.BlockSpec((bm, D), row),
                      pl.BlockSpec((1, D, D_EXPERT), wsel), pl.BlockSpec((1, D, D_EXPERT), wsel),
                      pl.BlockSpec((1, D_EXPERT, D), wsel)],
            out_specs=pl.BlockSpec((bm, D), row)),
        out_shape=jax.ShapeDtypeStruct((P, D), F32),
        compiler_params=pltpu.CompilerParams(dimension_semantics=("arbitrary",),
                                             vmem_limit_bytes=VMEM_LIMIT),
        name="experts",
    )(blk, xpad, wg, wu, wd)


def _combine_body(dest_ref, x1_ref, rg_ref, gfin_ref, ypad_ref, o_ref, buf0, buf1, sem):
    tmd = x1_ref.shape[0]

    def issue(i, _):
        _row_copy(ypad_ref, dest_ref[0, i], buf0, i, sem.at[0]).start()
        _row_copy(ypad_ref, dest_ref[1, i], buf1, i, sem.at[1]).start()
        return 0

    def drain(i, _):
        _row_copy(ypad_ref, 0, buf0, i, sem.at[0]).wait()
        _row_copy(ypad_ref, 0, buf1, i, sem.at[1]).wait()
        return 0

    lax.fori_loop(0, tmd, issue, 0)
    lax.fori_loop(0, tmd, drain, 0)
    gt = rg_ref[...].T
    y = x1_ref[...] + (gt[:, 0:1] * buf0[...] + gt[:, 1:2] * buf1[...])
    o_ref[...] = _rms(y, gfin_ref[...])


def _combine(dest, x1, rg, gfin, ypad, t):
    T, D = x1.shape
    tmd = t["tmd"]
    return pl.pallas_call(
        _combine_body,
        grid=(T // tmd,),
        in_specs=[pl.BlockSpec((2, tmd), lambda i: (0, i), memory_space=pltpu.SMEM),
                  pl.BlockSpec((tmd, D), lambda i: (i, 0)),
                  pl.BlockSpec((8, tmd), lambda i: (0, i)),
                  pl.BlockSpec((1, D), lambda i: (0, 0)),
                  pl.BlockSpec(memory_space=pl.ANY)],
        out_specs=pl.BlockSpec((tmd, D), lambda i: (i, 0)),
        out_shape=jax.ShapeDtypeStruct((T, D), F32),
        scratch_shapes=[pltpu.VMEM((tmd, D), F32), pltpu.VMEM((tmd, D), F32),
                        pltpu.SemaphoreType.DMA((2,))],
        compiler_params=pltpu.CompilerParams(dimension_semantics=("arbitrary",)),
        name="combine",
    )(dest, x1, rg, gfin, ypad)


def _pair_perm():
    idx = []
    for p in range(N_PAIRS):
        for l in range(LANES):
            hh, part, j = (l // 32) % 2, l // 64, l % 32
            idx.append((2 * p + hh) * DK_RET + part * 32 + j)
    return jnp.asarray(idx, I32)


def _spread_rope_cols(w):
    z = jnp.zeros(w.shape[:-1] + (32,), w.dtype)
    return jnp.concatenate([w[..., :32], z, w[..., 32:], z], axis=-1)


def _prep_weights(w_in, w_q_up, w_kv_up, w_out, w_rg, w_re, w_g, w_u, w_d):
    perm = _pair_perm()
    o = D_RET
    win = jnp.concatenate([
        w_in[:, 0:o][:, perm], w_in[:, o:2 * o][:, perm], w_in[:, 2 * o:4 * o],
        w_in[:, 4 * o:4 * o + MLA_Q_RANK + MLA_KV_RANK],
        _spread_rope_cols(w_in[:, 4 * o + MLA_Q_RANK + MLA_KV_RANK:])], axis=1).astype(BF16)
    wq = w_q_up.reshape(MLA_Q_RANK, H_MLA, MLA_NOPE + MLA_ROPE)
    wq = jnp.concatenate([wq[..., :MLA_NOPE], _spread_rope_cols(wq[..., MLA_NOPE:])], axis=-1)
    wq = wq.reshape(MLA_Q_RANK, H_MLA * D_QK_MLA).astype(BF16)
    wkv = w_kv_up.reshape(MLA_KV_RANK, H_MLA, MLA_NOPE + MLA_V)
    wkn = wkv[..., :MLA_NOPE].reshape(MLA_KV_RANK, H_MLA * MLA_NOPE).astype(BF16)
    wvt = wkv[..., MLA_NOPE:].reshape(MLA_KV_RANK, H_MLA * MLA_V).T.astype(BF16)
    wr = jnp.concatenate([
        w_rg, jnp.zeros((D_MODEL, 8 - N_GROUPS), F32), w_re,
        jnp.zeros((D_MODEL, ROUTER_COLS - 8 - N_EXPERTS), F32)], axis=1).astype(BF16)
    return dict(win=win, wq=wq, wkn=wkn, wvt=wvt, wo=w_out.astype(BF16), wr=wr,
                wg=w_g.astype(BF16), wu=w_u.astype(BF16), wd=w_d.astype(BF16))


def _rope_tables(seq):
    def tab(dim):
        inv = 1.0 / (ROPE_BASE ** (jnp.arange(0, dim, 2, dtype=F32) / dim))
        ang = jnp.arange(seq, dtype=F32)[:, None] * inv[None, :]
        return jnp.cos(ang), jnp.sin(ang)
    c, s = tab(DK_RET)
    cr = jnp.concatenate([c, c, c, c], axis=1)
    sr = jnp.concatenate([-s, -s, s, s], axis=1)
    c, s = tab(MLA_ROPE)
    z = jnp.zeros_like(c)
    cm = jnp.concatenate([c, z, c, z], axis=1)
    sm = jnp.concatenate([-s, z, s, z], axis=1)
    return cr, sr, cm, sm


def _retention_tables(log_decay):
    C = RET_CHUNK
    lg = -jnp.exp(log_decay.astype(F32))
    lf, lb = lg[0], lg[1]
    idx = jnp.arange(C, dtype=F32)
    lane = jnp.arange(D_RET)
    khead = 2 * (lane // LANES) + (lane % LANES // 32) % 2
    lfk, lbk = lf[khead][None, :], lb[khead][None, :]
    qfs = jnp.exp(lfk * (idx + 1.0)[:, None])
    qbs = jnp.exp(lbk * (C - idx)[:, None])
    kfs = jnp.exp(lfk * (C - 1 - idx)[:, None])
    kbs = jnp.exp(lbk * idx[:, None])
    diff = idx[:, None] - idx[None, :]
    d_f = jnp.where(diff >= 0, jnp.exp(lf[:, None, None] * jnp.maximum(diff, 0.0)), 0.0)
    d_b = jnp.where(diff < 0, jnp.exp(lb[:, None, None] * jnp.maximum(-diff, 0.0)), 0.0)
    dmat = (d_f + d_b).reshape(N_PAIRS, 2, C, C)
    dm = jnp.concatenate([dmat[:, 0], dmat[:, 1]], axis=-1)
    krow = jnp.arange(LANES)
    rhead = 2 * jnp.arange(N_PAIRS)[:, None] + ((krow // 32) % 2)[None, :]
    decf = jnp.broadcast_to(jnp.exp(lf * C)[rhead][:, :, None], (N_PAIRS, LANES, LANES))
    decb = jnp.broadcast_to(jnp.exp(lb * C)[rhead][:, :, None], (N_PAIRS, LANES, LANES))
    return qfs, qbs, kfs, kbs, dm, decf, decb


def _trunk(x, w, gmix, gq, gkv, gn, gffn, gfin, ret_tabs):
    B, S, D = x.shape
    T = B * S
    t = _tiles(S)
    cr, sr, cm, sm = _rope_tables(S)
    q, k, v, sg, qf, kf, vt = _inproj(x, gmix, w["win"], cr, sr, cm, sm, gq, w["wq"], gkv,
                                      w["wkn"], w["wvt"], t)
    ret = _retention(q, k, v, sg, ret_tabs, gn, t)
    mla = _flash(qf, kf, vt, t)
    x1, ri, rg, cnt = _outproj_router(x.reshape(T, D), ret.reshape(T, D_RET),
                                      mla.reshape(T, H_MLA * MLA_V), w["wo"], gffn, w["wr"], t)
    bm = t["bm"]
    P = 2 * T + N_EXPERTS * bm
    nb = P // bm
    dest, blk = _plan(ri, cnt, nb, t)
    xpad = _dispatch(dest, x1, gffn, jnp.zeros((P, D), F32), t)
    ypad = _experts(blk.reshape(-1), xpad, w["wg"], w["wu"], w["wd"], nb, t)
    out = _combine(dest, x1, rg, gfin, ypad, t)
    return out.reshape(B, S, D)


def kernel(x_prompt, x_sample, g_mix, w_in, ret_log_decay, ret_gn_g, mla_q_norm_g, w_q_up,
           mla_kv_norm_g, w_kv_up, w_out, g_ffn, w_router_group, w_router_expert,
           w_exp_gate, w_exp_up, w_exp_down, g_final):
    assert g_mix.shape[0] == 1, "single-layer trunk"
    w = _prep_weights(w_in[0], w_q_up[0], w_kv_up[0], w_out[0], w_router_group[0],
                      w_router_expert[0], w_exp_gate[0], w_exp_up[0], w_exp_down[0])
    ret_tabs = _retention_tables(ret_log_decay[0])
    row = lambda a: a.reshape(1, -1).astype(F32)
    args = (w, row(g_mix[0]), row(mla_q_norm_g[0]), row(mla_kv_norm_g[0]), row(ret_gn_g[0]),
            row(g_ffn[0]), row(g_final), ret_tabs)
    return (_trunk(x_prompt, *args), _trunk(x_sample, *args))
```

```python
import functools
import math

import jax
import jax.numpy as jnp
from jax import lax
from jax.experimental import pallas as pl
from jax.experimental.pallas import tpu as pltpu

F32 = jnp.float32
BF16 = jnp.bfloat16
I32 = jnp.int32

D_MODEL = 1024
H_RET = 8
DK_RET = 64
DV_RET = 64
RET_CHUNK = 128
H_MLA = 4
MLA_Q_RANK = 256
MLA_KV_RANK = 128
MLA_NOPE = 128
MLA_ROPE = 64
MLA_V = 128
ROPE_BASE = 10000.0
N_GROUPS = 4
EXPERTS_PER_GROUP = 8
N_EXPERTS = N_GROUPS * EXPERTS_PER_GROUP
D_EXPERT = 512
EPS = 1e-6

LANES = 128
N_PAIRS = H_RET // 2
D_RET = H_RET * DK_RET
D_QK_MLA = 2 * LANES
D_IN_PAD = 4 * D_RET + MLA_Q_RANK + MLA_KV_RANK + LANES
ROUTER_COLS = LANES
VMEM_LIMIT = 56 * 1024 * 1024


def _tiles(seq):
    tm = min(512, seq)
    return dict(
        tm=tm,
        tq=min(256, seq),
        tk=tm,
        ch=min(4, seq // RET_CHUNK),
        bm=256,
        tmd=min(256, seq),
        tp=min(2048, seq),
    )


def _mm(a, b):
    return jnp.dot(a, b, preferred_element_type=F32)


def _mm_nt(a, b):
    return lax.dot_general(a, b, (((1,), (1,)), ((), ())), preferred_element_type=F32)


def _rms(x, g):
    return x * lax.rsqrt(jnp.mean(x * x, axis=-1, keepdims=True) + EPS) * g


def _rope_half(x, c, s):
    return x * c + pltpu.roll(x, 64, 1) * s


def _inproj_body(x_ref, gmix_ref, win_ref, cr_ref, sr_ref, cm_ref, sm_ref, gq_ref, wq_ref,
                 gkv_ref, wkn_ref, wvt_ref,
                 q_ref, k_ref, v_ref, sg_ref, qf_ref, kf_ref, vt_ref):
    h = _rms(x_ref[0], gmix_ref[...]).astype(BF16)

    def proj(lo, hi):
        return _mm(h, win_ref[:, lo:hi])

    cr, sr = cr_ref[...], sr_ref[...]
    q = proj(0, D_RET)
    k = proj(D_RET, 2 * D_RET)
    for p in range(N_PAIRS):
        sl = slice(p * LANES, (p + 1) * LANES)
        q_ref[0, :, sl] = _rope_half(q[:, sl], cr, sr).astype(BF16)
        k_ref[0, :, sl] = (_rope_half(k[:, sl], cr, sr) * (DK_RET ** -0.5)).astype(BF16)
    v_ref[0] = proj(2 * D_RET, 3 * D_RET).astype(BF16)
    g = proj(3 * D_RET, 4 * D_RET)
    sg_ref[0] = (g * jax.nn.sigmoid(g)).astype(BF16)

    c = proj(4 * D_RET, D_IN_PAD)
    cq = c[:, :MLA_Q_RANK]
    ckv = c[:, MLA_Q_RANK:MLA_Q_RANK + MLA_KV_RANK]
    kr = c[:, MLA_Q_RANK + MLA_KV_RANK:]
    cm, sm = cm_ref[...], sm_ref[...]
    scale = (MLA_NOPE + MLA_ROPE) ** -0.5 * math.log2(math.e)

    qh = _mm(_rms(cq, gq_ref[...]).astype(BF16), wq_ref[...])
    for hd in range(H_MLA):
        base = hd * D_QK_MLA
        qf_ref[0, hd, :, :LANES] = (qh[:, base:base + LANES] * scale).astype(BF16)
        qr = _rope_half(qh[:, base + LANES:base + D_QK_MLA], cm, sm)
        qf_ref[0, hd, :, LANES:] = (qr * scale).astype(BF16)

    ckvn = _rms(ckv, gkv_ref[...]).astype(BF16)
    kn = _mm(ckvn, wkn_ref[...])
    krr = _rope_half(kr, cm, sm).astype(BF16)
    vt = _mm_nt(wvt_ref[...], ckvn)
    for hd in range(H_MLA):
        sl = slice(hd * LANES, (hd + 1) * LANES)
        kf_ref[0, hd, :, :LANES] = kn[:, sl].astype(BF16)
        kf_ref[0, hd, :, LANES:] = krr
        vt_ref[0, hd] = vt[sl, :].astype(BF16)


def _inproj(x, gmix, win, cr, sr, cm, sm, gq, wq, gkv, wkn, wvt, t):
    B, S, D = x.shape
    tm = t["tm"]
    ns = S // tm
    tok = lambda b, j: (b, j, 0)
    full = lambda b, j: (0, 0)
    tab = lambda b, j: (j, 0)
    head = lambda b, j: (b, 0, j, 0)
    out_shape = (
        jax.ShapeDtypeStruct((B, S, D_RET), BF16),
        jax.ShapeDtypeStruct((B, S, D_RET), BF16),
        jax.ShapeDtypeStruct((B, S, D_RET), BF16),
        jax.ShapeDtypeStruct((B, S, D_RET), BF16),
        jax.ShapeDtypeStruct((B, H_MLA, S, D_QK_MLA), BF16),
        jax.ShapeDtypeStruct((B, H_MLA, S, D_QK_MLA), BF16),
        jax.ShapeDtypeStruct((B, H_MLA, MLA_V, S), BF16),
    )
    return pl.pallas_call(
        _inproj_body,
        grid=(B, ns),
        in_specs=[
            pl.BlockSpec((1, tm, D), tok),
            pl.BlockSpec((1, D), full),
            pl.BlockSpec((D, D_IN_PAD), full),
            pl.BlockSpec((tm, LANES), tab), pl.BlockSpec((tm, LANES), tab),
            pl.BlockSpec((tm, LANES), tab), pl.BlockSpec((tm, LANES), tab),
            pl.BlockSpec((1, MLA_Q_RANK), full),
            pl.BlockSpec((MLA_Q_RANK, H_MLA * D_QK_MLA), full),
            pl.BlockSpec((1, MLA_KV_RANK), full),
            pl.BlockSpec((MLA_KV_RANK, H_MLA * MLA_NOPE), full),
            pl.BlockSpec((H_MLA * MLA_V, MLA_KV_RANK), full),
        ],
        out_specs=(
            pl.BlockSpec((1, tm, D_RET), tok), pl.BlockSpec((1, tm, D_RET), tok),
            pl.BlockSpec((1, tm, D_RET), tok), pl.BlockSpec((1, tm, D_RET), tok),
            pl.BlockSpec((1, H_MLA, tm, D_QK_MLA), head),
            pl.BlockSpec((1, H_MLA, tm, D_QK_MLA), head),
            pl.BlockSpec((1, H_MLA, MLA_V, tm), lambda b, j: (b, 0, 0, j)),
        ),
        out_shape=out_shape,
        compiler_params=pltpu.CompilerParams(
            dimension_semantics=("parallel", "parallel"), vmem_limit_bytes=VMEM_LIMIT),
        name="inproj",
    )(x, gmix, win, cr, sr, cm, sm, gq, wq, gkv, wkn, wvt)


def _pair_masks():
    lane = lax.broadcasted_iota(I32, (1, LANES), 1)
    khead = (lane // 32) % 2
    vhead = lane // DV_RET
    row = lax.broadcasted_iota(I32, (LANES, LANES), 0)
    col = lax.broadcasted_iota(I32, (LANES, LANES), 1)
    block = ((row // 32) % 2) == (col // DV_RET)
    return khead, vhead, block


def _chunk_state(kp, vp, kscale, block):
    kw = (kp.astype(F32) * kscale).T.astype(BF16)
    s = _mm(kw, vp)
    return jnp.where(block, s, 0.0)


def _ret_bwd_body(k_ref, v_ref, kbs_ref, decb_ref, stb_ref, carry_ref, *, ch):
    @pl.when(pl.program_id(1) == 0)
    def _():
        carry_ref[...] = jnp.zeros_like(carry_ref)

    _, _, block = _pair_masks()
    for cc in reversed(range(ch)):
        rows = slice(cc * RET_CHUNK, (cc + 1) * RET_CHUNK)
        for p in range(N_PAIRS):
            sl = slice(p * LANES, (p + 1) * LANES)
            stb_ref[0, cc, p] = carry_ref[p].astype(BF16)
            s_b = _chunk_state(k_ref[0, rows, sl], v_ref[0, rows, sl], kbs_ref[:, sl], block)
            carry_ref[p] = carry_ref[p] * decb_ref[p] + s_b


def _ret_fwd_body(q_ref, k_ref, v_ref, sg_ref, stb_ref, qfs_ref, qbs_ref, kfs_ref, dm_ref,
                  decf_ref, gn_ref, o_ref, carry_ref, *, ch):
    @pl.when(pl.program_id(1) == 0)
    def _():
        carry_ref[...] = jnp.zeros_like(carry_ref)

    khead, vhead, block = _pair_masks()
    zero_bf = jnp.zeros((), BF16)
    for cc in range(ch):
        rows = slice(cc * RET_CHUNK, (cc + 1) * RET_CHUNK)
        for p in range(N_PAIRS):
            sl = slice(p * LANES, (p + 1) * LANES)
            qp, kp, vp = q_ref[0, rows, sl], k_ref[0, rows, sl], v_ref[0, rows, sl]
            k2 = jnp.concatenate([jnp.where(khead == 0, kp, zero_bf),
                                  jnp.where(khead == 1, kp, zero_bf)], axis=0)
            sc = _mm_nt(qp, k2) * dm_ref[p]
            v2 = jnp.concatenate([jnp.where(vhead == 0, vp, zero_bf),
                                  jnp.where(vhead == 1, vp, zero_bf)], axis=0)
            intra = _mm(sc.astype(BF16), v2)
            qf = qp.astype(F32)
            qfb = jnp.concatenate([(qf * qfs_ref[:, sl]).astype(BF16),
                                   (qf * qbs_ref[:, sl]).astype(BF16)], axis=1)
            st2 = jnp.concatenate([carry_ref[p].astype(BF16), stb_ref[0, cc, p]], axis=0)
            o = intra + _mm(qfb, st2)
            m0 = vhead == 0
            mu = jnp.where(m0,
                           jnp.sum(jnp.where(m0, o, 0.0), axis=-1, keepdims=True),
                           jnp.sum(jnp.where(m0, 0.0, o), axis=-1, keepdims=True)) * (1.0 / DV_RET)
            d = o - mu
            d2 = d * d
            var = jnp.where(m0,
                            jnp.sum(jnp.where(m0, d2, 0.0), axis=-1, keepdims=True),
                            jnp.sum(jnp.where(m0, 0.0, d2), axis=-1, keepdims=True)) * (1.0 / DV_RET)
            y = d * lax.rsqrt(var + EPS) * gn_ref[:, sl]
            o_ref[0, rows, sl] = (sg_ref[0, rows, sl].astype(F32) * y).astype(BF16)
            s_f = _chunk_state(kp, vp, kfs_ref[:, sl], block)
            carry_ref[p] = carry_ref[p] * decf_ref[p] + s_f


def _retention(q, k, v, sg, tabs, gn, t):
    B, S, _ = q.shape
    ch = t["ch"]
    rows = ch * RET_CHUNK
    ns = S // rows
    n_chunks = S // RET_CHUNK
    qfs, qbs, kfs, kbs, dm, decf, decb = tabs
    tab2 = pl.BlockSpec((RET_CHUNK, D_RET), lambda b, j: (0, 0))
    tab3 = pl.BlockSpec((N_PAIRS, LANES, LANES), lambda b, j: (0, 0, 0))
    params = pltpu.CompilerParams(dimension_semantics=("parallel", "arbitrary"),
                                  vmem_limit_bytes=VMEM_LIMIT)
    rev = lambda b, j: (b, ns - 1 - j, 0)
    stb = pl.pallas_call(
        functools.partial(_ret_bwd_body, ch=ch),
        grid=(B, ns),
        in_specs=[pl.BlockSpec((1, rows, D_RET), rev), pl.BlockSpec((1, rows, D_RET), rev),
                  tab2, tab3],
        out_specs=pl.BlockSpec((1, ch, N_PAIRS, LANES, LANES),
                               lambda b, j: (b, ns - 1 - j, 0, 0, 0)),
        out_shape=jax.ShapeDtypeStruct((B, n_chunks, N_PAIRS, LANES, LANES), BF16),
        scratch_shapes=[pltpu.VMEM((N_PAIRS, LANES, LANES), F32)],
        compiler_params=params,
        name="ret_bwd",
    )(k, v, kbs, decb)
    fwd = lambda b, j: (b, j, 0)
    tok = pl.BlockSpec((1, rows, D_RET), fwd)
    return pl.pallas_call(
        functools.partial(_ret_fwd_body, ch=ch),
        grid=(B, ns),
        in_specs=[tok, tok, tok, tok,
                  pl.BlockSpec((1, ch, N_PAIRS, LANES, LANES), lambda b, j: (b, j, 0, 0, 0)),
                  tab2, tab2, tab2,
                  pl.BlockSpec((N_PAIRS, RET_CHUNK, 2 * RET_CHUNK), lambda b, j: (0, 0, 0)),
                  tab3,
                  pl.BlockSpec((1, D_RET), lambda b, j: (0, 0))],
        out_specs=tok,
        out_shape=jax.ShapeDtypeStruct((B, S, D_RET), BF16),
        scratch_shapes=[pltpu.VMEM((N_PAIRS, LANES, LANES), F32)],
        compiler_params=params,
        name="ret_fwd",
    )(q, k, v, sg, stb, qfs, qbs, kfs, dm, decf, gn)


def _flash_body(q_ref, k_ref, vt_ref, o_ref, s0_ref, s1_ref, m0_ref, m1_ref, *, sub):
    n = q_ref.shape[2] // sub

    def scores(i, s_ref, m_ref):
        s = _mm_nt(k_ref[0, 0], q_ref[0, 0, pl.ds(pl.multiple_of(i * sub, sub), sub), :])
        s_ref[...] = s
        m_ref[...] = jnp.max(s, axis=0, keepdims=True)

    def finish(i, s_ref, m_ref):
        p = jnp.exp2(s_ref[...] - m_ref[...])
        l = jnp.sum(p, axis=0, keepdims=True)
        o = _mm(vt_ref[0, 0], p.astype(BF16))
        o_ref[0, pl.ds(pl.multiple_of(i * sub, sub), sub), :] = (o / l).T.astype(BF16)

    scores(0, s0_ref, m0_ref)

    def pair(j, carry):
        i = 2 * j
        scores(i + 1, s1_ref, m1_ref)
        finish(i, s0_ref, m0_ref)
        scores(i + 2, s0_ref, m0_ref)
        finish(i + 1, s1_ref, m1_ref)
        return carry

    lax.fori_loop(0, n // 2 - 1, pair, 0)
    scores(n - 1, s1_ref, m1_ref)
    finish(n - 2, s0_ref, m0_ref)
    finish(n - 1, s1_ref, m1_ref)


def _flash(qf, kf, vt, t):
    B, H, S, _ = qf.shape
    sub = t["tq"]
    assert (S // sub) % 2 == 0
    whole = lambda b, h: (b, h, 0, 0)
    return pl.pallas_call(
        functools.partial(_flash_body, sub=sub),
        grid=(B, H),
        in_specs=[pl.BlockSpec((1, 1, S, D_QK_MLA), whole),
                  pl.BlockSpec((1, 1, S, D_QK_MLA), whole),
                  pl.BlockSpec((1, 1, MLA_V, S), whole)],
        out_specs=pl.BlockSpec((1, S, MLA_V), lambda b, h: (b, 0, h)),
        out_shape=jax.ShapeDtypeStruct((B, S, H * MLA_V), BF16),
        scratch_shapes=[pltpu.VMEM((S, sub), F32), pltpu.VMEM((S, sub), F32),
                        pltpu.VMEM((1, sub), F32), pltpu.VMEM((1, sub), F32)],
        compiler_params=pltpu.CompilerParams(
            dimension_semantics=("parallel", "parallel"), vmem_limit_bytes=VMEM_LIMIT),
        name="flash",
    )(qf, kf, vt)


def _outproj_router_body(x_ref, ret_ref, mla_ref, wo_ref, gffn_ref, wr_ref,
                         x1_ref, ri_ref, rg_ref, cnt_ref, tri_ref, carry_ref):
    tm = x_ref.shape[0]

    @pl.when(pl.program_id(0) == 0)
    def _():
        carry_ref[...] = jnp.zeros_like(carry_ref)
        r = lax.broadcasted_iota(I32, (tm, tm), 0)
        c = lax.broadcasted_iota(I32, (tm, tm), 1)
        tri_ref[...] = (r < c).astype(BF16)

    x1 = (x_ref[...]
          + _mm(ret_ref[...], wo_ref[:D_RET, :]) + _mm(mla_ref[...], wo_ref[D_RET:, :]))
    x1_ref[...] = x1
    h2 = _rms(x1, gffn_ref[...]).astype(BF16)
    lt = _mm(h2, wr_ref[...]).T

    row8 = lax.broadcasted_iota(I32, (8, tm), 0)
    lg = jnp.where(row8 < N_GROUPS, lt[0:8], -jnp.inf)
    eg = jnp.exp(lg - jnp.max(lg, axis=0, keepdims=True))
    pg = eg / jnp.sum(eg, axis=0, keepdims=True)
    p_sel = jnp.max(pg, axis=0, keepdims=True)
    g_sel = jnp.min(jnp.where(pg == p_sel, row8, 8), axis=0, keepdims=True)

    le = lt[8:16]
    for g in range(1, N_GROUPS):
        le = jnp.where(g_sel == g, lt[8 + 8 * g:16 + 8 * g], le)
    ee = jnp.exp(le - jnp.max(le, axis=0, keepdims=True))
    pe = ee / jnp.sum(ee, axis=0, keepdims=True)
    p1 = jnp.max(pe, axis=0, keepdims=True)
    i1 = jnp.min(jnp.where(pe == p1, row8, 8), axis=0, keepdims=True)
    pe2 = jnp.where(row8 == i1, -1.0, pe)
    p2 = jnp.max(pe2, axis=0, keepdims=True)
    i2 = jnp.min(jnp.where(pe2 == p2, row8, 8), axis=0, keepdims=True)
    den = p1 + p2
    e0 = g_sel * EXPERTS_PER_GROUP + i1
    e1 = g_sel * EXPERTS_PER_GROUP + i2

    rowe = lax.broadcasted_iota(I32, (N_EXPERTS, tm), 0)
    oh0 = rowe == e0
    oh1 = rowe == e1
    cnt = jnp.where(oh0 | oh1, 1.0, 0.0)
    base = carry_ref[:, 0:1] + _mm(cnt.astype(BF16), tri_ref[...])
    r0 = jnp.sum(jnp.where(oh0, base, 0.0), axis=0, keepdims=True)
    r1 = jnp.sum(jnp.where(oh1, base, 0.0), axis=0, keepdims=True)
    carry_ref[...] = carry_ref[...] + jnp.sum(cnt, axis=1, keepdims=True)

    zi = jnp.zeros((4, tm), I32)
    ri_ref[...] = jnp.concatenate([e0, e1, r0.astype(I32), r1.astype(I32), zi], axis=0)
    zf = jnp.zeros((6, tm), F32)
    rg_ref[...] = jnp.concatenate([p_sel * (p1 / den), p_sel * (p2 / den), zf], axis=0)
    cnt_ref[...] = carry_ref[...].astype(I32)


def _outproj_router(x2, ret2, mla2, wo, gffn, wr, t):
    T, D = x2.shape
    tm = t["tm"]
    tok = lambda i: (i, 0)
    full = lambda i: (0, 0)
    lane = lambda i: (0, i)
    return pl.pallas_call(
        _outproj_router_body,
        grid=(T // tm,),
        in_specs=[pl.BlockSpec((tm, D), tok), pl.BlockSpec((tm, D_RET), tok),
                  pl.BlockSpec((tm, H_MLA * MLA_V), tok),
                  pl.BlockSpec((D, D), full), pl.BlockSpec((1, D), full),
                  pl.BlockSpec((D, ROUTER_COLS), full)],
        out_specs=(pl.BlockSpec((tm, D), tok), pl.BlockSpec((8, tm), lane),
                   pl.BlockSpec((8, tm), lane), pl.BlockSpec((N_EXPERTS, LANES), full)),
        out_shape=(jax.ShapeDtypeStruct((T, D), F32), jax.ShapeDtypeStruct((8, T), I32),
                   jax.ShapeDtypeStruct((8, T), F32), jax.ShapeDtypeStruct((N_EXPERTS, LANES), I32)),
        scratch_shapes=[pltpu.VMEM((tm, tm), BF16), pltpu.VMEM((N_EXPERTS, LANES), F32)],
        compiler_params=pltpu.CompilerParams(dimension_semantics=("arbitrary",),
                                             vmem_limit_bytes=VMEM_LIMIT),
        name="outproj_router",
    )(x2, ret2, mla2, wo, gffn, wr)


def _plan_body(ri_ref, cnt_ref, dest_ref, blk_ref, *, bm, nb_pad):
    tp = ri_ref.shape[1]
    shift = int(math.log2(bm))
    cnt_col = cnt_ref[:, 0:1]
    pc_col = (((cnt_col + (bm - 1)) >> shift) << shift).astype(F32)
    r = lax.broadcasted_iota(I32, (N_EXPERTS, N_EXPERTS), 0)
    c = lax.broadcasted_iota(I32, (N_EXPERTS, N_EXPERTS), 1)
    pc_row = jnp.sum(jnp.where(r == c, pc_col, 0.0), axis=0, keepdims=True)
    pstart_col = jnp.sum(jnp.where(c < r, pc_row, 0.0), axis=1, keepdims=True)
    pend_col = pstart_col + pc_col

    rowe = lax.broadcasted_iota(I32, (N_EXPERTS, tp), 0)
    for kk in range(2):
        e = ri_ref[kk:kk + 1, :]
        off = jnp.sum(jnp.where(rowe == e, pstart_col, 0.0), axis=0, keepdims=True)
        dest_ref[kk:kk + 1, :] = off.astype(I32) + ri_ref[2 + kk:3 + kk, :]

    blk_row = (lax.broadcasted_iota(I32, (N_EXPERTS, nb_pad), 1) * bm).astype(F32)
    owner = jnp.sum(jnp.where(pend_col <= blk_row, 1.0, 0.0), axis=0, keepdims=True)
    owner = jnp.minimum(owner, N_EXPERTS - 1.0).astype(I32)
    n_used = jnp.max(pend_col, axis=0, keepdims=True).astype(I32) >> shift
    lane = lax.broadcasted_iota(I32, (1, nb_pad), 1)
    blk_ref[...] = jnp.where(lane == nb_pad - 1, n_used, owner)


def _plan(ri, cnt, nb, t):
    T = ri.shape[1]
    tp, bm = t["tp"], t["bm"]
    nb_pad = (nb + 1 + LANES - 1) // LANES * LANES
    return pl.pallas_call(
        functools.partial(_plan_body, bm=bm, nb_pad=nb_pad),
        grid=(T // tp,),
        in_specs=[pl.BlockSpec((8, tp), lambda i: (0, i)),
                  pl.BlockSpec((N_EXPERTS, LANES), lambda i: (0, 0))],
        out_specs=(pl.BlockSpec((2, tp), lambda i: (0, i)),
                   pl.BlockSpec((1, nb_pad), lambda i: (0, 0))),
        out_shape=(jax.ShapeDtypeStruct((2, T), I32), jax.ShapeDtypeStruct((1, nb_pad), I32)),
        compiler_params=pltpu.CompilerParams(dimension_semantics=("arbitrary",)),
        name="plan",
    )(ri, cnt)


ROW_UNROLL = 8


def _row_copy(src, src_row, dst, dst_row, sem):
    return pltpu.make_async_copy(src.at[pl.ds(src_row, 1), :], dst.at[pl.ds(dst_row, 1), :], sem)


def _for_rows(n, fn):
    def group(g, carry):
        for j in range(ROW_UNROLL):
            fn(g * ROW_UNROLL + j)
        return carry
    lax.fori_loop(0, n // ROW_UNROLL, group, 0)


def _dispatch_body(dest_ref, x1_ref, gffn_ref, xpad_in_ref, xpad_ref, hbuf, sem):
    del xpad_in_ref
    tmd = x1_ref.shape[0]
    hbuf[...] = _rms(x1_ref[...], gffn_ref[...])

    def issue(i):
        for kk in range(2):
            _row_copy(hbuf, i, xpad_ref, dest_ref[kk, i], sem.at[kk]).start(priority=kk)

    def drain(i):
        for kk in range(2):
            _row_copy(hbuf, i, xpad_ref, 0, sem.at[kk]).wait()

    _for_rows(tmd, issue)
    _for_rows(tmd, drain)


def _dispatch(dest, x1, gffn, xpad0, t):
    T, D = x1.shape
    tmd = t["tmd"]
    return pl.pallas_call(
        _dispatch_body,
        grid=(T // tmd,),
        in_specs=[pl.BlockSpec((2, tmd), lambda i: (0, i), memory_space=pltpu.SMEM),
                  pl.BlockSpec((tmd, D), lambda i: (i, 0)),
                  pl.BlockSpec((1, D), lambda i: (0, 0)),
                  pl.BlockSpec(memory_space=pl.ANY)],
        out_specs=pl.BlockSpec(memory_space=pl.ANY),
        out_shape=jax.ShapeDtypeStruct(xpad0.shape, F32),
        scratch_shapes=[pltpu.VMEM((tmd, D), F32), pltpu.SemaphoreType.DMA((2,))],
        input_output_aliases={3: 0},
        compiler_params=pltpu.CompilerParams(dimension_semantics=("arbitrary",)),
        name="dispatch",
    )(dest, x1, gffn, xpad0)


def _experts_body(blk_ref, x_ref, wg_ref, wu_ref, wd_ref, y_ref, *, n_used_idx):
    @pl.when(pl.program_id(0) < blk_ref[n_used_idx])
    def _():
        xb = x_ref[...].astype(BF16)
        a = _mm(xb, wg_ref[0])
        u = _mm(xb, wu_ref[0])
        hmid = (a * jax.nn.sigmoid(a) * u).astype(BF16)
        y_ref[...] = _mm(hmid, wd_ref[0])

    @pl.when(pl.program_id(0) >= blk_ref[n_used_idx])
    def _():
        y_ref[...] = jnp.zeros_like(y_ref)


def _experts(blk, xpad, wg, wu, wd, nb, t):
    P, D = xpad.shape
    bm = t["bm"]
    n_used_idx = blk.shape[0] - 1
    row = lambda j, blk: (j, 0)
    wsel = lambda j, blk: (blk[j], 0, 0)
    return pl.pallas_call(
        functools.partial(_experts_body, n_used_idx=n_used_idx),
        grid_spec=pltpu.PrefetchScalarGridSpec(
            num_scalar_prefetch=1,
            grid=(nb,),
            in_specs=[pl.BlockSpec((bm, D), row),
                      pl.BlockSpec((1, D, D_EXPERT), wsel), pl.BlockSpec((1, D, D_EXPERT), wsel),
                      pl.BlockSpec((1, D_EXPERT, D), wsel)],
            out_specs=pl.BlockSpec((bm, D), row)),
        out_shape=jax.ShapeDtypeStruct((P, D), F32),
        compiler_params=pltpu.CompilerParams(dimension_semantics=("arbitrary",),
                                             vmem_limit_bytes=VMEM_LIMIT),
        name="experts",
    )(blk, xpad, wg, wu, wd)


def _combine_body(dest_ref, x1_ref, rg_ref, gfin_ref, ypad_ref, o_ref, buf0, buf1, sem):
    tmd = x1_ref.shape[0]
    bufs = (buf0, buf1)

    def issue(i):
        for kk in range(2):
            _row_copy(ypad_ref, dest_ref[kk, i], bufs[kk], i, sem.at[kk]).start(priority=kk)

    def drain(i):
        for kk in range(2):
            _row_copy(ypad_ref, 0, bufs[kk], i, sem.at[kk]).wait()

    _for_rows(tmd, issue)
    _for_rows(tmd, drain)
    gt = rg_ref[...].T
    y = x1_ref[...] + (gt[:, 0:1] * buf0[...] + gt[:, 1:2] * buf1[...])
    o_ref[...] = _rms(y, gfin_ref[...])


def _combine(dest, x1, rg, gfin, ypad, t):
    T, D = x1.shape
    tmd = t["tmd"]
    return pl.pallas_call(
        _combine_body,
        grid=(T // tmd,),
        in_specs=[pl.BlockSpec((2, tmd), lambda i: (0, i), memory_space=pltpu.SMEM),
                  pl.BlockSpec((tmd, D), lambda i: (i, 0)),
                  pl.BlockSpec((8, tmd), lambda i: (0, i)),
                  pl.BlockSpec((1, D), lambda i: (0, 0)),
                  pl.BlockSpec(memory_space=pl.ANY)],
        out_specs=pl.BlockSpec((tmd, D), lambda i: (i, 0)),
        out_shape=jax.ShapeDtypeStruct((T, D), F32),
        scratch_shapes=[pltpu.VMEM((tmd, D), F32), pltpu.VMEM((tmd, D), F32),
                        pltpu.SemaphoreType.DMA((2,))],
        compiler_params=pltpu.CompilerParams(dimension_semantics=("arbitrary",)),
        name="combine",
    )(dest, x1, rg, gfin, ypad)


def _pair_perm():
    idx = []
    for p in range(N_PAIRS):
        for l in range(LANES):
            hh, part, j = (l // 32) % 2, l // 64, l % 32
            idx.append((2 * p + hh) * DK_RET + part * 32 + j)
    return jnp.asarray(idx, I32)


def _spread_rope_cols(w):
    z = jnp.zeros(w.shape[:-1] + (32,), w.dtype)
    return jnp.concatenate([w[..., :32], z, w[..., 32:], z], axis=-1)


def _prep_weights(w_in, w_q_up, w_kv_up, w_out, w_rg, w_re, w_g, w_u, w_d):
    perm = _pair_perm()
    o = D_RET
    win = jnp.concatenate([
        w_in[:, 0:o][:, perm], w_in[:, o:2 * o][:, perm], w_in[:, 2 * o:4 * o],
        w_in[:, 4 * o:4 * o + MLA_Q_RANK + MLA_KV_RANK],
        _spread_rope_cols(w_in[:, 4 * o + MLA_Q_RANK + MLA_KV_RANK:])], axis=1).astype(BF16)
    wq = w_q_up.reshape(MLA_Q_RANK, H_MLA, MLA_NOPE + MLA_ROPE)
    wq = jnp.concatenate([wq[..., :MLA_NOPE], _spread_rope_cols(wq[..., MLA_NOPE:])], axis=-1)
    wq = wq.reshape(MLA_Q_RANK, H_MLA * D_QK_MLA).astype(BF16)
    wkv = w_kv_up.reshape(MLA_KV_RANK, H_MLA, MLA_NOPE + MLA_V)
    wkn = wkv[..., :MLA_NOPE].reshape(MLA_KV_RANK, H_MLA * MLA_NOPE).astype(BF16)
    wvt = wkv[..., MLA_NOPE:].reshape(MLA_KV_RANK, H_MLA * MLA_V).T.astype(BF16)
    wr = jnp.concatenate([
        w_rg, jnp.zeros((D_MODEL, 8 - N_GROUPS), F32), w_re,
        jnp.zeros((D_MODEL, ROUTER_COLS - 8 - N_EXPERTS), F32)], axis=1).astype(BF16)
    return dict(win=win, wq=wq, wkn=wkn, wvt=wvt, wo=w_out.astype(BF16), wr=wr,
                wg=w_g.astype(BF16), wu=w_u.astype(BF16), wd=w_d.astype(BF16))


def _rope_tables(seq):
    def tab(dim):
        inv = 1.0 / (ROPE_BASE ** (jnp.arange(0, dim, 2, dtype=F32) / dim))
        ang = jnp.arange(seq, dtype=F32)[:, None] * inv[None, :]
        return jnp.cos(ang), jnp.sin(ang)
    c, s = tab(DK_RET)
    cr = jnp.concatenate([c, c, c, c], axis=1)
    sr = jnp.concatenate([-s, -s, s, s], axis=1)
    c, s = tab(MLA_ROPE)
    z = jnp.zeros_like(c)
    cm = jnp.concatenate([c, z, c, z], axis=1)
    sm = jnp.concatenate([-s, z, s, z], axis=1)
    return cr, sr, cm, sm


def _retention_tables(log_decay):
    C = RET_CHUNK
    lg = -jnp.exp(log_decay.astype(F32))
    lf, lb = lg[0], lg[1]
    idx = jnp.arange(C, dtype=F32)
    lane = jnp.arange(D_RET)
    khead = 2 * (lane // LANES) + (lane % LANES // 32) % 2
    lfk, lbk = lf[khead][None, :], lb[khead][None, :]
    qfs = jnp.exp(lfk * (idx + 1.0)[:, None])
    qbs = jnp.exp(lbk * (C - idx)[:, None])
    kfs = jnp.exp(lfk * (C - 1 - idx)[:, None])
    kbs = jnp.exp(lbk * idx[:, None])
    diff = idx[:, None] - idx[None, :]
    d_f = jnp.where(diff >= 0, jnp.exp(lf[:, None, None] * jnp.maximum(diff, 0.0)), 0.0)
    d_b = jnp.where(diff < 0, jnp.exp(lb[:, None, None] * jnp.maximum(-diff, 0.0)), 0.0)
    dmat = (d_f + d_b).reshape(N_PAIRS, 2, C, C)
    dm = jnp.concatenate([dmat[:, 0], dmat[:, 1]], axis=-1)
    krow = jnp.arange(LANES)
    rhead = 2 * jnp.arange(N_PAIRS)[:, None] + ((krow // 32) % 2)[None, :]
    decf = jnp.broadcast_to(jnp.exp(lf * C)[rhead][:, :, None], (N_PAIRS, LANES, LANES))
    decb = jnp.broadcast_to(jnp.exp(lb * C)[rhead][:, :, None], (N_PAIRS, LANES, LANES))
    return qfs, qbs, kfs, kbs, dm, decf, decb


def _trunk(x, w, gmix, gq, gkv, gn, gffn, gfin, ret_tabs):
    B, S, D = x.shape
    T = B * S
    t = _tiles(S)
    cr, sr, cm, sm = _rope_tables(S)
    q, k, v, sg, qf, kf, vt = _inproj(x, gmix, w["win"], cr, sr, cm, sm, gq, w["wq"], gkv,
                                      w["wkn"], w["wvt"], t)
    ret = _retention(q, k, v, sg, ret_tabs, gn, t)
    mla = _flash(qf, kf, vt, t)
    x1, ri, rg, cnt = _outproj_router(x.reshape(T, D), ret.reshape(T, D_RET),
                                      mla.reshape(T, H_MLA * MLA_V), w["wo"], gffn, w["wr"], t)
    bm = t["bm"]
    P = 2 * T + N_EXPERTS * bm
    nb = P // bm
    dest, blk = _plan(ri, cnt, nb, t)
    xpad = _dispatch(dest, x1, gffn, jnp.zeros((P, D), F32), t)
    ypad = _experts(blk.reshape(-1), xpad, w["wg"], w["wu"], w["wd"], nb, t)
    out = _combine(dest, x1, rg, gfin, ypad, t)
    return out.reshape(B, S, D)


def kernel(x_prompt, x_sample, g_mix, w_in, ret_log_decay, ret_gn_g, mla_q_norm_g, w_q_up,
           mla_kv_norm_g, w_kv_up, w_out, g_ffn, w_router_group, w_router_expert,
           w_exp_gate, w_exp_up, w_exp_down, g_final):
    assert g_mix.shape[0] == 1, "single-layer trunk"
    w = _prep_weights(w_in[0], w_q_up[0], w_kv_up[0], w_out[0], w_router_group[0],
                      w_router_expert[0], w_exp_gate[0], w_exp_up[0], w_exp_down[0])
    ret_tabs = _retention_tables(ret_log_decay[0])
    row = lambda a: a.reshape(1, -1).astype(F32)
    args = (w, row(g_mix[0]), row(mla_q_norm_g[0]), row(mla_kv_norm_g[0]), row(ret_gn_g[0]),
            row(g_ffn[0]), row(g_final), ret_tabs)
    return (_trunk(x_prompt, *args), _trunk(x_sample, *args))
```

```python
import functools
import math

import jax
import jax.numpy as jnp
from jax import lax
from jax.experimental import pallas as pl
from jax.experimental.pallas import tpu as pltpu

F32 = jnp.float32
BF16 = jnp.bfloat16
I32 = jnp.int32

D_MODEL = 1024
H_RET = 8
DK_RET = 64
DV_RET = 64
RET_CHUNK = 128
H_MLA = 4
MLA_Q_RANK = 256
MLA_KV_RANK = 128
MLA_NOPE = 128
MLA_ROPE = 64
MLA_V = 128
ROPE_BASE = 10000.0
N_GROUPS = 4
EXPERTS_PER_GROUP = 8
N_EXPERTS = N_GROUPS * EXPERTS_PER_GROUP
D_EXPERT = 512
EPS = 1e-6

LANES = 128
N_PAIRS = H_RET // 2
D_RET = H_RET * DK_RET
D_QK_MLA = 2 * LANES
D_IN_PAD = 4 * D_RET + MLA_Q_RANK + MLA_KV_RANK + LANES
ROUTER_COLS = LANES
VMEM_LIMIT = 56 * 1024 * 1024


def _tiles(seq):
    tm = min(512, seq)
    return dict(
        tm=tm,
        tq=min(512 if seq <= 4096 else 256, seq // 2),
        tk=tm,
        ch=min(4, seq // RET_CHUNK),
        bm=256,
        tmd=min(512, seq),
        tp=min(2048, seq),
    )


def _mm(a, b):
    return jnp.dot(a, b, preferred_element_type=F32)


def _mm_nt(a, b):
    return lax.dot_general(a, b, (((1,), (1,)), ((), ())), preferred_element_type=F32)


def _rms(x, g):
    return x * lax.rsqrt(jnp.mean(x * x, axis=-1, keepdims=True) + EPS) * g


def _rope_half(x, c, s):
    return x * c + pltpu.roll(x, 64, 1) * s


def _inproj_body(x_ref, gmix_ref, win_ref, cr_ref, sr_ref, cm_ref, sm_ref, gq_ref, wq_ref,
                 gkv_ref, wkn_ref, wvt_ref,
                 q_ref, k_ref, v_ref, sg_ref, qf_ref, kf_ref, vt_ref):
    h = _rms(x_ref[0], gmix_ref[...]).astype(BF16)

    def proj(lo, hi):
        return _mm(h, win_ref[:, lo:hi])

    cr, sr = cr_ref[...], sr_ref[...]
    q = proj(0, D_RET)
    k = proj(D_RET, 2 * D_RET)
    for p in range(N_PAIRS):
        sl = slice(p * LANES, (p + 1) * LANES)
        q_ref[0, :, sl] = _rope_half(q[:, sl], cr, sr).astype(BF16)
        k_ref[0, :, sl] = (_rope_half(k[:, sl], cr, sr) * (DK_RET ** -0.5)).astype(BF16)
    v_ref[0] = proj(2 * D_RET, 3 * D_RET).astype(BF16)
    g = proj(3 * D_RET, 4 * D_RET)
    sg_ref[0] = (g * jax.nn.sigmoid(g)).astype(BF16)

    c = proj(4 * D_RET, D_IN_PAD)
    cq = c[:, :MLA_Q_RANK]
    ckv = c[:, MLA_Q_RANK:MLA_Q_RANK + MLA_KV_RANK]
    kr = c[:, MLA_Q_RANK + MLA_KV_RANK:]
    cm, sm = cm_ref[...], sm_ref[...]
    scale = (MLA_NOPE + MLA_ROPE) ** -0.5 * math.log2(math.e)

    qh = _mm(_rms(cq, gq_ref[...]).astype(BF16), wq_ref[...])
    for hd in range(H_MLA):
        base = hd * D_QK_MLA
        qf_ref[0, hd, :, :LANES] = (qh[:, base:base + LANES] * scale).astype(BF16)
        qr = _rope_half(qh[:, base + LANES:base + D_QK_MLA], cm, sm)
        qf_ref[0, hd, :, LANES:] = (qr * scale).astype(BF16)

    ckvn = _rms(ckv, gkv_ref[...]).astype(BF16)
    kn = _mm(ckvn, wkn_ref[...])
    krr = _rope_half(kr, cm, sm).astype(BF16)
    vt = _mm_nt(wvt_ref[...], ckvn)
    for hd in range(H_MLA):
        sl = slice(hd * LANES, (hd + 1) * LANES)
        kf_ref[0, hd, :, :LANES] = kn[:, sl].astype(BF16)
        kf_ref[0, hd, :, LANES:] = krr
        vt_ref[0, hd] = vt[sl, :].astype(BF16)


def _inproj(x, gmix, win, cr, sr, cm, sm, gq, wq, gkv, wkn, wvt, t):
    B, S, D = x.shape
    tm = t["tm"]
    ns = S // tm
    tok = lambda b, j: (b, j, 0)
    full = lambda b, j: (0, 0)
    tab = lambda b, j: (j, 0)
    head = lambda b, j: (b, 0, j, 0)
    out_shape = (
        jax.ShapeDtypeStruct((B, S, D_RET), BF16),
        jax.ShapeDtypeStruct((B, S, D_RET), BF16),
        jax.ShapeDtypeStruct((B, S, D_RET), BF16),
        jax.ShapeDtypeStruct((B, S, D_RET), BF16),
        jax.ShapeDtypeStruct((B, H_MLA, S, D_QK_MLA), BF16),
        jax.ShapeDtypeStruct((B, H_MLA, S, D_QK_MLA), BF16),
        jax.ShapeDtypeStruct((B, H_MLA, MLA_V, S), BF16),
    )
    return pl.pallas_call(
        _inproj_body,
        grid=(B, ns),
        in_specs=[
            pl.BlockSpec((1, tm, D), tok),
            pl.BlockSpec((1, D), full),
            pl.BlockSpec((D, D_IN_PAD), full),
            pl.BlockSpec((tm, LANES), tab), pl.BlockSpec((tm, LANES), tab),
            pl.BlockSpec((tm, LANES), tab), pl.BlockSpec((tm, LANES), tab),
            pl.BlockSpec((1, MLA_Q_RANK), full),
            pl.BlockSpec((MLA_Q_RANK, H_MLA * D_QK_MLA), full),
            pl.BlockSpec((1, MLA_KV_RANK), full),
            pl.BlockSpec((MLA_KV_RANK, H_MLA * MLA_NOPE), full),
            pl.BlockSpec((H_MLA * MLA_V, MLA_KV_RANK), full),
        ],
        out_specs=(
            pl.BlockSpec((1, tm, D_RET), tok), pl.BlockSpec((1, tm, D_RET), tok),
            pl.BlockSpec((1, tm, D_RET), tok), pl.BlockSpec((1, tm, D_RET), tok),
            pl.BlockSpec((1, H_MLA, tm, D_QK_MLA), head),
            pl.BlockSpec((1, H_MLA, tm, D_QK_MLA), head),
            pl.BlockSpec((1, H_MLA, MLA_V, tm), lambda b, j: (b, 0, 0, j)),
        ),
        out_shape=out_shape,
        compiler_params=pltpu.CompilerParams(
            dimension_semantics=("parallel", "parallel"), vmem_limit_bytes=VMEM_LIMIT),
        name="inproj",
    )(x, gmix, win, cr, sr, cm, sm, gq, wq, gkv, wkn, wvt)


def _pair_masks():
    lane = lax.broadcasted_iota(I32, (1, LANES), 1)
    khead = (lane // 32) % 2
    vhead = lane // DV_RET
    row = lax.broadcasted_iota(I32, (LANES, LANES), 0)
    col = lax.broadcasted_iota(I32, (LANES, LANES), 1)
    block = ((row // 32) % 2) == (col // DV_RET)
    return khead, vhead, block


def _chunk_state(kp, vp, kscale, block):
    kw = (kp.astype(F32) * kscale).T.astype(BF16)
    s = _mm(kw, vp)
    return jnp.where(block, s, 0.0)


def _ret_bwd_body(k_ref, v_ref, kbs_ref, decb_ref, stb_ref, carry_ref, *, ch):
    @pl.when(pl.program_id(1) == 0)
    def _():
        carry_ref[...] = jnp.zeros_like(carry_ref)

    _, _, block = _pair_masks()
    for cc in reversed(range(ch)):
        rows = slice(cc * RET_CHUNK, (cc + 1) * RET_CHUNK)
        for p in range(N_PAIRS):
            sl = slice(p * LANES, (p + 1) * LANES)
            stb_ref[0, cc, p] = carry_ref[p].astype(BF16)
            s_b = _chunk_state(k_ref[0, rows, sl], v_ref[0, rows, sl], kbs_ref[:, sl], block)
            carry_ref[p] = carry_ref[p] * decb_ref[p] + s_b


def _ret_fwd_body(q_ref, k_ref, v_ref, sg_ref, stb_ref, qfs_ref, qbs_ref, kfs_ref, dm_ref,
                  decf_ref, gn_ref, o_ref, carry_ref, *, ch):
    @pl.when(pl.program_id(1) == 0)
    def _():
        carry_ref[...] = jnp.zeros_like(carry_ref)

    khead, vhead, block = _pair_masks()
    zero_bf = jnp.zeros((), BF16)
    for cc in range(ch):
        rows = slice(cc * RET_CHUNK, (cc + 1) * RET_CHUNK)
        for p in range(N_PAIRS):
            sl = slice(p * LANES, (p + 1) * LANES)
            qp, kp, vp = q_ref[0, rows, sl], k_ref[0, rows, sl], v_ref[0, rows, sl]
            k2 = jnp.concatenate([jnp.where(khead == 0, kp, zero_bf),
                                  jnp.where(khead == 1, kp, zero_bf)], axis=0)
            sc = _mm_nt(qp, k2) * dm_ref[p]
            v2 = jnp.concatenate([jnp.where(vhead == 0, vp, zero_bf),
                                  jnp.where(vhead == 1, vp, zero_bf)], axis=0)
            intra = _mm(sc.astype(BF16), v2)
            qf = qp.astype(F32)
            qfb = jnp.concatenate([(qf * qfs_ref[:, sl]).astype(BF16),
                                   (qf * qbs_ref[:, sl]).astype(BF16)], axis=1)
            st2 = jnp.concatenate([carry_ref[p].astype(BF16), stb_ref[0, cc, p]], axis=0)
            o = intra + _mm(qfb, st2)
            m0 = vhead == 0
            mu = jnp.where(m0,
                           jnp.sum(jnp.where(m0, o, 0.0), axis=-1, keepdims=True),
                           jnp.sum(jnp.where(m0, 0.0, o), axis=-1, keepdims=True)) * (1.0 / DV_RET)
            d = o - mu
            d2 = d * d
            var = jnp.where(m0,
                            jnp.sum(jnp.where(m0, d2, 0.0), axis=-1, keepdims=True),
                            jnp.sum(jnp.where(m0, 0.0, d2), axis=-1, keepdims=True)) * (1.0 / DV_RET)
            y = d * lax.rsqrt(var + EPS) * gn_ref[:, sl]
            o_ref[0, rows, sl] = (sg_ref[0, rows, sl].astype(F32) * y).astype(BF16)
            s_f = _chunk_state(kp, vp, kfs_ref[:, sl], block)
            carry_ref[p] = carry_ref[p] * decf_ref[p] + s_f


def _retention(q, k, v, sg, tabs, gn, t):
    B, S, _ = q.shape
    ch = t["ch"]
    rows = ch * RET_CHUNK
    ns = S // rows
    n_chunks = S // RET_CHUNK
    qfs, qbs, kfs, kbs, dm, decf, decb = tabs
    tab2 = pl.BlockSpec((RET_CHUNK, D_RET), lambda b, j: (0, 0))
    tab3 = pl.BlockSpec((N_PAIRS, LANES, LANES), lambda b, j: (0, 0, 0))
    params = pltpu.CompilerParams(dimension_semantics=("parallel", "arbitrary"),
                                  vmem_limit_bytes=VMEM_LIMIT)
    rev = lambda b, j: (b, ns - 1 - j, 0)
    stb = pl.pallas_call(
        functools.partial(_ret_bwd_body, ch=ch),
        grid=(B, ns),
        in_specs=[pl.BlockSpec((1, rows, D_RET), rev), pl.BlockSpec((1, rows, D_RET), rev),
                  tab2, tab3],
        out_specs=pl.BlockSpec((1, ch, N_PAIRS, LANES, LANES),
                               lambda b, j: (b, ns - 1 - j, 0, 0, 0)),
        out_shape=jax.ShapeDtypeStruct((B, n_chunks, N_PAIRS, LANES, LANES), BF16),
        scratch_shapes=[pltpu.VMEM((N_PAIRS, LANES, LANES), F32)],
        compiler_params=params,
        name="ret_bwd",
    )(k, v, kbs, decb)
    fwd = lambda b, j: (b, j, 0)
    tok = pl.BlockSpec((1, rows, D_RET), fwd)
    return pl.pallas_call(
        functools.partial(_ret_fwd_body, ch=ch),
        grid=(B, ns),
        in_specs=[tok, tok, tok, tok,
                  pl.BlockSpec((1, ch, N_PAIRS, LANES, LANES), lambda b, j: (b, j, 0, 0, 0)),
                  tab2, tab2, tab2,
                  pl.BlockSpec((N_PAIRS, RET_CHUNK, 2 * RET_CHUNK), lambda b, j: (0, 0, 0)),
                  tab3,
                  pl.BlockSpec((1, D_RET), lambda b, j: (0, 0))],
        out_specs=tok,
        out_shape=jax.ShapeDtypeStruct((B, S, D_RET), BF16),
        scratch_shapes=[pltpu.VMEM((N_PAIRS, LANES, LANES), F32)],
        compiler_params=params,
        name="ret_fwd",
    )(q, k, v, sg, stb, qfs, qbs, kfs, dm, decf, gn)


def _flash_body(q_ref, k_ref, vt_ref, o_ref, s0_ref, s1_ref, m0_ref, m1_ref, *, sub):
    n = q_ref.shape[2] // sub

    def scores(i, s_ref, m_ref):
        s = _mm_nt(k_ref[0, 0], q_ref[0, 0, pl.ds(pl.multiple_of(i * sub, sub), sub), :])
        s_ref[...] = s
        m_ref[...] = jnp.max(s, axis=0, keepdims=True)

    def finish(i, s_ref, m_ref):
        p = jnp.exp2(s_ref[...] - m_ref[...])
        l = jnp.sum(p, axis=0, keepdims=True)
        o = _mm(vt_ref[0, 0], p.astype(BF16))
        o_ref[0, pl.ds(pl.multiple_of(i * sub, sub), sub), :] = (o / l).T.astype(BF16)

    scores(0, s0_ref, m0_ref)

    def pair(j, carry):
        i = 2 * j
        scores(i + 1, s1_ref, m1_ref)
        finish(i, s0_ref, m0_ref)
        scores(i + 2, s0_ref, m0_ref)
        finish(i + 1, s1_ref, m1_ref)
        return carry

    lax.fori_loop(0, n // 2 - 1, pair, 0)
    scores(n - 1, s1_ref, m1_ref)
    finish(n - 2, s0_ref, m0_ref)
    finish(n - 1, s1_ref, m1_ref)


def _flash(qf, kf, vt, t):
    B, H, S, _ = qf.shape
    sub = t["tq"]
    assert (S // sub) % 2 == 0
    whole = lambda b, h: (b, h, 0, 0)
    return pl.pallas_call(
        functools.partial(_flash_body, sub=sub),
        grid=(B, H),
        in_specs=[pl.BlockSpec((1, 1, S, D_QK_MLA), whole),
                  pl.BlockSpec((1, 1, S, D_QK_MLA), whole),
                  pl.BlockSpec((1, 1, MLA_V, S), whole)],
        out_specs=pl.BlockSpec((1, S, MLA_V), lambda b, h: (b, 0, h)),
        out_shape=jax.ShapeDtypeStruct((B, S, H * MLA_V), BF16),
        scratch_shapes=[pltpu.VMEM((S, sub), F32), pltpu.VMEM((S, sub), F32),
                        pltpu.VMEM((1, sub), F32), pltpu.VMEM((1, sub), F32)],
        compiler_params=pltpu.CompilerParams(
            dimension_semantics=("parallel", "parallel"), vmem_limit_bytes=VMEM_LIMIT),
        name="flash",
    )(qf, kf, vt)


def _outproj_router_body(x_ref, ret_ref, mla_ref, wo_ref, gffn_ref, wr_ref,
                         x1_ref, ri_ref, rg_ref, cnt_ref, tri_ref, carry_ref):
    tm = x_ref.shape[0]

    @pl.when(pl.program_id(0) == 0)
    def _():
        carry_ref[...] = jnp.zeros_like(carry_ref)
        r = lax.broadcasted_iota(I32, (tm, tm), 0)
        c = lax.broadcasted_iota(I32, (tm, tm), 1)
        tri_ref[...] = (r < c).astype(BF16)

    x1 = (x_ref[...]
          + _mm(ret_ref[...], wo_ref[:D_RET, :]) + _mm(mla_ref[...], wo_ref[D_RET:, :]))
    x1_ref[...] = x1
    h2 = _rms(x1, gffn_ref[...]).astype(BF16)
    lt = _mm(h2, wr_ref[...]).T

    row8 = lax.broadcasted_iota(I32, (8, tm), 0)
    lg = jnp.where(row8 < N_GROUPS, lt[0:8], -jnp.inf)
    eg = jnp.exp(lg - jnp.max(lg, axis=0, keepdims=True))
    pg = eg / jnp.sum(eg, axis=0, keepdims=True)
    p_sel = jnp.max(pg, axis=0, keepdims=True)
    g_sel = jnp.min(jnp.where(pg == p_sel, row8, 8), axis=0, keepdims=True)

    le = lt[8:16]
    for g in range(1, N_GROUPS):
        le = jnp.where(g_sel == g, lt[8 + 8 * g:16 + 8 * g], le)
    ee = jnp.exp(le - jnp.max(le, axis=0, keepdims=True))
    pe = ee / jnp.sum(ee, axis=0, keepdims=True)
    p1 = jnp.max(pe, axis=0, keepdims=True)
    i1 = jnp.min(jnp.where(pe == p1, row8, 8), axis=0, keepdims=True)
    pe2 = jnp.where(row8 == i1, -1.0, pe)
    p2 = jnp.max(pe2, axis=0, keepdims=True)
    i2 = jnp.min(jnp.where(pe2 == p2, row8, 8), axis=0, keepdims=True)
    den = p1 + p2
    e0 = g_sel * EXPERTS_PER_GROUP + i1
    e1 = g_sel * EXPERTS_PER_GROUP + i2

    rowe = lax.broadcasted_iota(I32, (N_EXPERTS, tm), 0)
    oh0 = rowe == e0
    oh1 = rowe == e1
    cnt = jnp.where(oh0 | oh1, 1.0, 0.0)
    base = carry_ref[:, 0:1] + _mm(cnt.astype(BF16), tri_ref[...])
    r0 = jnp.sum(jnp.where(oh0, base, 0.0), axis=0, keepdims=True)
    r1 = jnp.sum(jnp.where(oh1, base, 0.0), axis=0, keepdims=True)
    carry_ref[...] = carry_ref[...] + jnp.sum(cnt, axis=1, keepdims=True)

    zi = jnp.zeros((4, tm), I32)
    ri_ref[...] = jnp.concatenate([e0, e1, r0.astype(I32), r1.astype(I32), zi], axis=0)
    zf = jnp.zeros((6, tm), F32)
    rg_ref[...] = jnp.concatenate([p_sel * (p1 / den), p_sel * (p2 / den), zf], axis=0)
    cnt_ref[...] = carry_ref[...].astype(I32)


def _outproj_router(x2, ret2, mla2, wo, gffn, wr, t):
    T, D = x2.shape
    tm = t["tm"]
    tok = lambda i: (i, 0)
    full = lambda i: (0, 0)
    lane = lambda i: (0, i)
    return pl.pallas_call(
        _outproj_router_body,
        grid=(T // tm,),
        in_specs=[pl.BlockSpec((tm, D), tok), pl.BlockSpec((tm, D_RET), tok),
                  pl.BlockSpec((tm, H_MLA * MLA_V), tok),
                  pl.BlockSpec((D, D), full), pl.BlockSpec((1, D), full),
                  pl.BlockSpec((D, ROUTER_COLS), full)],
        out_specs=(pl.BlockSpec((tm, D), tok), pl.BlockSpec((8, tm), lane),
                   pl.BlockSpec((8, tm), lane), pl.BlockSpec((N_EXPERTS, LANES), full)),
        out_shape=(jax.ShapeDtypeStruct((T, D), F32), jax.ShapeDtypeStruct((8, T), I32),
                   jax.ShapeDtypeStruct((8, T), F32), jax.ShapeDtypeStruct((N_EXPERTS, LANES), I32)),
        scratch_shapes=[pltpu.VMEM((tm, tm), BF16), pltpu.VMEM((N_EXPERTS, LANES), F32)],
        compiler_params=pltpu.CompilerParams(dimension_semantics=("arbitrary",),
                                             vmem_limit_bytes=VMEM_LIMIT),
        name="outproj_router",
    )(x2, ret2, mla2, wo, gffn, wr)


def _plan_body(ri_ref, cnt_ref, dest_ref, blk_ref, *, bm, nb_pad):
    tp = ri_ref.shape[1]
    shift = int(math.log2(bm))
    cnt_col = cnt_ref[:, 0:1]
    pc_col = (((cnt_col + (bm - 1)) >> shift) << shift).astype(F32)
    r = lax.broadcasted_iota(I32, (N_EXPERTS, N_EXPERTS), 0)
    c = lax.broadcasted_iota(I32, (N_EXPERTS, N_EXPERTS), 1)
    pc_row = jnp.sum(jnp.where(r == c, pc_col, 0.0), axis=0, keepdims=True)
    pstart_col = jnp.sum(jnp.where(c < r, pc_row, 0.0), axis=1, keepdims=True)
    pend_col = pstart_col + pc_col

    rowe = lax.broadcasted_iota(I32, (N_EXPERTS, tp), 0)
    for kk in range(2):
        e = ri_ref[kk:kk + 1, :]
        off = jnp.sum(jnp.where(rowe == e, pstart_col, 0.0), axis=0, keepdims=True)
        dest_ref[kk:kk + 1, :] = off.astype(I32) + ri_ref[2 + kk:3 + kk, :]

    blk_row = (lax.broadcasted_iota(I32, (N_EXPERTS, nb_pad), 1) * bm).astype(F32)
    owner = jnp.sum(jnp.where(pend_col <= blk_row, 1.0, 0.0), axis=0, keepdims=True)
    owner = jnp.minimum(owner, N_EXPERTS - 1.0).astype(I32)
    n_used = jnp.max(pend_col, axis=0, keepdims=True).astype(I32) >> shift
    lane = lax.broadcasted_iota(I32, (1, nb_pad), 1)
    blk_ref[...] = jnp.where(lane == nb_pad - 1, n_used, owner)


def _plan(ri, cnt, nb, t):
    T = ri.shape[1]
    tp, bm = t["tp"], t["bm"]
    nb_pad = (nb + 1 + LANES - 1) // LANES * LANES
    return pl.pallas_call(
        functools.partial(_plan_body, bm=bm, nb_pad=nb_pad),
        grid=(T // tp,),
        in_specs=[pl.BlockSpec((8, tp), lambda i: (0, i)),
                  pl.BlockSpec((N_EXPERTS, LANES), lambda i: (0, 0))],
        out_specs=(pl.BlockSpec((2, tp), lambda i: (0, i)),
                   pl.BlockSpec((1, nb_pad), lambda i: (0, 0))),
        out_shape=(jax.ShapeDtypeStruct((2, T), I32), jax.ShapeDtypeStruct((1, nb_pad), I32)),
        compiler_params=pltpu.CompilerParams(dimension_semantics=("arbitrary",)),
        name="plan",
    )(ri, cnt)


ROW_UNROLL = 8


def _row_copy(src, src_row, dst, dst_row, sem):
    return pltpu.make_async_copy(src.at[pl.ds(src_row, 1), :], dst.at[pl.ds(dst_row, 1), :], sem)


def _for_rows(n, fn):
    def group(g, carry):
        for j in range(ROW_UNROLL):
            fn(g * ROW_UNROLL + j)
        return carry
    lax.fori_loop(0, n // ROW_UNROLL, group, 0)


def _dispatch_body(dest_ref, x1_ref, gffn_ref, xpad_in_ref, xpad_ref, hbuf, sem):
    del xpad_in_ref
    tmd = x1_ref.shape[0]
    hbuf[...] = _rms(x1_ref[...], gffn_ref[...])

    def issue(i):
        for kk in range(2):
            _row_copy(hbuf, i, xpad_ref, dest_ref[kk, i], sem.at[kk]).start(priority=kk)

    def drain(i):
        for kk in range(2):
            _row_copy(hbuf, i, xpad_ref, 0, sem.at[kk]).wait()

    _for_rows(tmd, issue)
    _for_rows(tmd, drain)


def _dispatch(dest, x1, gffn, xpad0, t):
    T, D = x1.shape
    tmd = t["tmd"]
    return pl.pallas_call(
        _dispatch_body,
        grid=(T // tmd,),
        in_specs=[pl.BlockSpec((2, tmd), lambda i: (0, i), memory_space=pltpu.SMEM),
                  pl.BlockSpec((tmd, D), lambda i: (i, 0)),
                  pl.BlockSpec((1, D), lambda i: (0, 0)),
                  pl.BlockSpec(memory_space=pl.ANY)],
        out_specs=pl.BlockSpec(memory_space=pl.ANY),
        out_shape=jax.ShapeDtypeStruct(xpad0.shape, F32),
        scratch_shapes=[pltpu.VMEM((tmd, D), F32), pltpu.SemaphoreType.DMA((2,))],
        input_output_aliases={3: 0},
        compiler_params=pltpu.CompilerParams(dimension_semantics=("arbitrary",)),
        name="dispatch",
    )(dest, x1, gffn, xpad0)


def _experts_body(blk_ref, x_ref, wg_ref, wu_ref, wd_ref, y_ref, *, n_used_idx):
    @pl.when(pl.program_id(0) < blk_ref[n_used_idx])
    def _():
        xb = x_ref[...].astype(BF16)
        a = _mm(xb, wg_ref[0])
        u = _mm(xb, wu_ref[0])
        hmid = (a * jax.nn.sigmoid(a) * u).astype(BF16)
        y_ref[...] = _mm(hmid, wd_ref[0])

    @pl.when(pl.program_id(0) >= blk_ref[n_used_idx])
    def _():
        y_ref[...] = jnp.zeros_like(y_ref)


def _experts(blk, xpad, wg, wu, wd, nb, t):
    P, D = xpad.shape
    bm = t["bm"]
    n_used_idx = blk.shape[0] - 1
    row = lambda j, blk: (j, 0)
    wsel = lambda j, blk: (blk[j], 0, 0)
    return pl.pallas_call(
        functools.partial(_experts_body, n_used_idx=n_used_idx),
        grid_spec=pltpu.PrefetchScalarGridSpec(
            num_scalar_prefetch=1,
            grid=(nb,),
            in_specs=[pl.BlockSpec((bm, D), row),
                      pl.BlockSpec((1, D, D_EXPERT), wsel), pl.BlockSpec((1, D, D_EXPERT), wsel),
                      pl.BlockSpec((1, D_EXPERT, D), wsel)],
            out_specs=pl.BlockSpec((bm, D), row)),
        out_shape=jax.ShapeDtypeStruct((P, D), F32),
        compiler_params=pltpu.CompilerParams(dimension_semantics=("arbitrary",),
                                             vmem_limit_bytes=VMEM_LIMIT),
        name="experts",
    )(blk, xpad, wg, wu, wd)


def _combine_body(dest_ref, x1_ref, rg_ref, gfin_ref, ypad_ref, o_ref, buf0, buf1, sem):
    tmd = x1_ref.shape[0]
    bufs = (buf0, buf1)

    def issue(i):
        for kk in range(2):
            _row_copy(ypad_ref, dest_ref[kk, i], bufs[kk], i, sem.at[kk]).start(priority=kk)

    def drain(i):
        for kk in range(2):
            _row_copy(ypad_ref, 0, bufs[kk], i, sem.at[kk]).wait()

    _for_rows(tmd, issue)
    _for_rows(tmd, drain)
    gt = rg_ref[...].T
    y = x1_ref[...] + (gt[:, 0:1] * buf0[...] + gt[:, 1:2] * buf1[...])
    o_ref[...] = _rms(y, gfin_ref[...])


def _combine(dest, x1, rg, gfin, ypad, t):
    T, D = x1.shape
    tmd = t["tmd"]
    return pl.pallas_call(
        _combine_body,
        grid=(T // tmd,),
        in_specs=[pl.BlockSpec((2, tmd), lambda i: (0, i), memory_space=pltpu.SMEM),
                  pl.BlockSpec((tmd, D), lambda i: (i, 0)),
                  pl.BlockSpec((8, tmd), lambda i: (0, i)),
                  pl.BlockSpec((1, D), lambda i: (0, 0)),
                  pl.BlockSpec(memory_space=pl.ANY)],
        out_specs=pl.BlockSpec((tmd, D), lambda i: (i, 0)),
        out_shape=jax.ShapeDtypeStruct((T, D), F32),
        scratch_shapes=[pltpu.VMEM((tmd, D), F32), pltpu.VMEM((tmd, D), F32),
                        pltpu.SemaphoreType.DMA((2,))],
        compiler_params=pltpu.CompilerParams(dimension_semantics=("arbitrary",)),
        name="combine",
    )(dest, x1, rg, gfin, ypad)


def _pair_perm():
    idx = []
    for p in range(N_PAIRS):
        for l in range(LANES):
            hh, part, j = (l // 32) % 2, l // 64, l % 32
            idx.append((2 * p + hh) * DK_RET + part * 32 + j)
    return jnp.asarray(idx, I32)


def _spread_rope_cols(w):
    z = jnp.zeros(w.shape[:-1] + (32,), w.dtype)
    return jnp.concatenate([w[..., :32], z, w[..., 32:], z], axis=-1)


def _prep_weights(w_in, w_q_up, w_kv_up, w_out, w_rg, w_re, w_g, w_u, w_d):
    perm = _pair_perm()
    o = D_RET
    win = jnp.concatenate([
        w_in[:, 0:o][:, perm], w_in[:, o:2 * o][:, perm], w_in[:, 2 * o:4 * o],
        w_in[:, 4 * o:4 * o + MLA_Q_RANK + MLA_KV_RANK],
        _spread_rope_cols(w_in[:, 4 * o + MLA_Q_RANK + MLA_KV_RANK:])], axis=1).astype(BF16)
    wq = w_q_up.reshape(MLA_Q_RANK, H_MLA, MLA_NOPE + MLA_ROPE)
    wq = jnp.concatenate([wq[..., :MLA_NOPE], _spread_rope_cols(wq[..., MLA_NOPE:])], axis=-1)
    wq = wq.reshape(MLA_Q_RANK, H_MLA * D_QK_MLA).astype(BF16)
    wkv = w_kv_up.reshape(MLA_KV_RANK, H_MLA, MLA_NOPE + MLA_V)
    wkn = wkv[..., :MLA_NOPE].reshape(MLA_KV_RANK, H_MLA * MLA_NOPE).astype(BF16)
    wvt = wkv[..., MLA_NOPE:].reshape(MLA_KV_RANK, H_MLA * MLA_V).T.astype(BF16)
    wr = jnp.concatenate([
        w_rg, jnp.zeros((D_MODEL, 8 - N_GROUPS), F32), w_re,
        jnp.zeros((D_MODEL, ROUTER_COLS - 8 - N_EXPERTS), F32)], axis=1).astype(BF16)
    return dict(win=win, wq=wq, wkn=wkn, wvt=wvt, wo=w_out.astype(BF16), wr=wr,
                wg=w_g.astype(BF16), wu=w_u.astype(BF16), wd=w_d.astype(BF16))


def _rope_tables(seq):
    def tab(dim):
        inv = 1.0 / (ROPE_BASE ** (jnp.arange(0, dim, 2, dtype=F32) / dim))
        ang = jnp.arange(seq, dtype=F32)[:, None] * inv[None, :]
        return jnp.cos(ang), jnp.sin(ang)
    c, s = tab(DK_RET)
    cr = jnp.concatenate([c, c, c, c], axis=1)
    sr = jnp.concatenate([-s, -s, s, s], axis=1)
    c, s = tab(MLA_ROPE)
    z = jnp.zeros_like(c)
    cm = jnp.concatenate([c, z, c, z], axis=1)
    sm = jnp.concatenate([-s, z, s, z], axis=1)
    return cr, sr, cm, sm


def _retention_tables(log_decay):
    C = RET_CHUNK
    lg = -jnp.exp(log_decay.astype(F32))
    lf, lb = lg[0], lg[1]
    idx = jnp.arange(C, dtype=F32)
    lane = jnp.arange(D_RET)
    khead = 2 * (lane // LANES) + (lane % LANES // 32) % 2
    lfk, lbk = lf[khead][None, :], lb[khead][None, :]
    qfs = jnp.exp(lfk * (idx + 1.0)[:, None])
    qbs = jnp.exp(lbk * (C - idx)[:, None])
    kfs = jnp.exp(lfk * (C - 1 - idx)[:, None])
    kbs = jnp.exp(lbk * idx[:, None])
    diff = idx[:, None] - idx[None, :]
    d_f = jnp.where(diff >= 0, jnp.exp(lf[:, None, None] * jnp.maximum(diff, 0.0)), 0.0)
    d_b = jnp.where(diff < 0, jnp.exp(lb[:, None, None] * jnp.maximum(-diff, 0.0)), 0.0)
    dmat = (d_f + d_b).reshape(N_PAIRS, 2, C, C)
    dm = jnp.concatenate([dmat[:, 0], dmat[:, 1]], axis=-1)
    krow = jnp.arange(LANES)
    rhead = 2 * jnp.arange(N_PAIRS)[:, None] + ((krow // 32) % 2)[None, :]
    decf = jnp.broadcast_to(jnp.exp(lf * C)[rhead][:, :, None], (N_PAIRS, LANES, LANES))
    decb = jnp.broadcast_to(jnp.exp(lb * C)[rhead][:, :, None], (N_PAIRS, LANES, LANES))
    return qfs, qbs, kfs, kbs, dm, decf, decb


def _trunk(x, w, gmix, gq, gkv, gn, gffn, gfin, ret_tabs):
    B, S, D = x.shape
    T = B * S
    t = _tiles(S)
    cr, sr, cm, sm = _rope_tables(S)
    q, k, v, sg, qf, kf, vt = _inproj(x, gmix, w["win"], cr, sr, cm, sm, gq, w["wq"], gkv,
                                      w["wkn"], w["wvt"], t)
    ret = _retention(q, k, v, sg, ret_tabs, gn, t)
    mla = _flash(qf, kf, vt, t)
    x1, ri, rg, cnt = _outproj_router(x.reshape(T, D), ret.reshape(T, D_RET),
                                      mla.reshape(T, H_MLA * MLA_V), w["wo"], gffn, w["wr"], t)
    bm = t["bm"]
    P = 2 * T + N_EXPERTS * bm
    nb = P // bm
    dest, blk = _plan(ri, cnt, nb, t)
    xpad = _dispatch(dest, x1, gffn, jnp.zeros((P, D), F32), t)
    ypad = _experts(blk.reshape(-1), xpad, w["wg"], w["wu"], w["wd"], nb, t)
    out = _combine(dest, x1, rg, gfin, ypad, t)
    return out.reshape(B, S, D)


def kernel(x_prompt, x_sample, g_mix, w_in, ret_log_decay, ret_gn_g, mla_q_norm_g, w_q_up,
           mla_kv_norm_g, w_kv_up, w_out, g_ffn, w_router_group, w_router_expert,
           w_exp_gate, w_exp_up, w_exp_down, g_final):
    assert g_mix.shape[0] == 1, "single-layer trunk"
    w = _prep_weights(w_in[0], w_q_up[0], w_kv_up[0], w_out[0], w_router_group[0],
                      w_router_expert[0], w_exp_gate[0], w_exp_up[0], w_exp_down[0])
    ret_tabs = _retention_tables(ret_log_decay[0])
    row = lambda a: a.reshape(1, -1).astype(F32)
    args = (w, row(g_mix[0]), row(mla_q_norm_g[0]), row(mla_kv_norm_g[0]), row(ret_gn_g[0]),
            row(g_ffn[0]), row(g_final), ret_tabs)
    return (_trunk(x_prompt, *args), _trunk(x_sample, *args))
```
